```python
import jax, jax.numpy as jnp
from jax import lax
import numpy as np

D_MODEL = 2048
BATCH = 4
SEQ = 2048
DEPTH = 1
DEC_BATCH = 128
DEC_SEQ = 4
PAST_LEN = 8192
PAGE_SIZE = 128

FOX_HEADS = 8
FOX_KV_HEADS = 4
FOX_GROUP = FOX_HEADS // FOX_KV_HEADS
HEAD_DIM = 128
MLA_HEADS = 8
MLA_NOPE = 128
MLA_ROPE = 64
MLA_V = 128
Q_LORA = 512
KV_LORA = 512
ROPE_THETA = 10000.0
MIX_WIDTH = FOX_HEADS * HEAD_DIM + MLA_HEADS * MLA_V
Q_BLOCK = 128
PEER_HEADS = 8
N_KEYS = 128
N_EXPERTS = N_KEYS * N_KEYS
PEER_TOPK = 16
PEER_QDIM = 256
PEER_HALF = PEER_QDIM // 2
TOK_BLOCK = 128
RMS_EPS = 1e-6
FORGET_BIAS_INIT = 3.0
CACHE_SPARE_DIV = 4

kernel_name = 'fox_mla_peer_hybrid_step'


def _in_sizes():
    return (FOX_HEADS * HEAD_DIM, FOX_KV_HEADS * HEAD_DIM, FOX_KV_HEADS * HEAD_DIM, FOX_HEADS,
            Q_LORA, KV_LORA, MLA_ROPE, MIX_WIDTH)


def _rmsnorm(x, g):
    xf = x.astype(jnp.float32)
    y = xf * lax.rsqrt(jnp.mean(xf * xf, axis=-1, keepdims=True) + RMS_EPS)
    return (y * g.astype(jnp.float32)).astype(x.dtype)


def _rope(x, pos):
    half = MLA_ROPE // 2
    inv = ROPE_THETA ** (-jnp.arange(half, dtype=jnp.float32) / half)
    ang = pos.astype(jnp.float32)[:, None] * inv[None, :]
    shape = (pos.shape[0],) + (1,) * (x.ndim - 3) + (half,)
    cos, sin = jnp.cos(ang).reshape(shape), jnp.sin(ang).reshape(shape)
    xf = x.astype(jnp.float32)
    x1, x2 = xf[..., :half], xf[..., half:]
    return jnp.concatenate([x1 * cos - x2 * sin, x1 * sin + x2 * cos], axis=-1).astype(x.dtype)


def _adaln(c, w_ada, b_ada):
    mod = jax.nn.silu(c.astype(jnp.float32)) @ w_ada + b_ada
    return jnp.split(mod[:, None, :], 6, axis=-1)


def _mixer_in(h, pos, w_in, b_forget, g_cq, w_uq, g_ckv, w_uk):
    lead = h.shape[:-1]
    proj = h @ w_in
    offs = np.cumsum(_in_sizes())[:-1].tolist()
    q_f, k_f, v_f, f_logit, cq, ckv, kr, gate_logit = jnp.split(proj, offs, axis=-1)
    q_f = q_f.reshape(*lead, FOX_HEADS, HEAD_DIM)
    k_f = k_f.reshape(*lead, FOX_KV_HEADS, HEAD_DIM)
    v_f = v_f.reshape(*lead, FOX_KV_HEADS, HEAD_DIM)
    logf = jax.nn.log_sigmoid((f_logit + b_forget).astype(jnp.float32))
    q_m = (_rmsnorm(cq, g_cq) @ w_uq).reshape(*lead, MLA_HEADS, MLA_NOPE + MLA_ROPE)
    q_rope = _rope(q_m[..., MLA_NOPE:], pos)
    q_lat = jnp.einsum('...hn,chn->...hc', q_m[..., :MLA_NOPE], w_uk)
    c_kv = _rmsnorm(ckv, g_ckv)
    k_rope = _rope(kr, pos)
    return q_f, k_f, v_f, logf, q_lat, q_rope, c_kv, k_rope, gate_logit


def _mixer_out(o_fox, o_lat, gate_logit, w_uv, w_o):
    lead = o_fox.shape[:-2]
    o_mla = jnp.einsum('...hc,chn->...hn', o_lat, w_uv)
    merged = jnp.concatenate([o_fox.reshape(*lead, -1), o_mla.reshape(*lead, -1)], axis=-1)
    return (merged * jax.nn.sigmoid(gate_logit.astype(jnp.float32))) @ w_o


def _fox_prompt(q, k, v, logf):
    B, S = q.shape[:2]
    nb = S // Q_BLOCK
    scale = HEAD_DIM ** -0.5
    qb_all = q.reshape(B, nb, Q_BLOCK, FOX_KV_HEADS, FOX_GROUP, HEAD_DIM).swapaxes(0, 1)
    F = jnp.cumsum(logf, axis=1).reshape(B, S, FOX_KV_HEADS, FOX_GROUP).transpose(0, 2, 3, 1)
    Fq_all = F.reshape(B, FOX_KV_HEADS, FOX_GROUP, nb, Q_BLOCK).transpose(3, 0, 1, 2, 4)
    kf, vf = k.astype(jnp.float32), v.astype(jnp.float32)
    key_pos = jnp.arange(S)

    def block(args):
        qb, fq, start = args
        s = jnp.einsum('btjgd,bsjd->bjgts', qb.astype(jnp.float32), kf) * scale + fq[..., :, None] - F[..., None, :]
        q_pos = start + jnp.arange(Q_BLOCK)
        s = jnp.where(key_pos[None, :] <= q_pos[:, None], s, -jnp.inf)
        p = jax.nn.softmax(s, axis=-1)
        return jnp.einsum('bjgts,bsjd->btjgd', p, vf)

    o = lax.map(block, (qb_all, Fq_all, jnp.arange(nb) * Q_BLOCK))
    return o.swapaxes(0, 1).reshape(B, S, FOX_HEADS, HEAD_DIM)


def _mla_prompt(q_lat, q_rope, c_kv, k_rope):
    B, S = q_lat.shape[:2]
    nb = S // Q_BLOCK
    scale = (MLA_NOPE + MLA_ROPE) ** -0.5
    ql_all = q_lat.reshape(B, nb, Q_BLOCK, MLA_HEADS, KV_LORA).swapaxes(0, 1)
    qr_all = q_rope.reshape(B, nb, Q_BLOCK, MLA_HEADS, MLA_ROPE).swapaxes(0, 1)
    ckv, kr = c_kv.astype(jnp.float32), k_rope.astype(jnp.float32)
    key_pos = jnp.arange(S)

    def block(args):
        qlb, qrb, start = args
        s = (jnp.einsum('bthc,bsc->bhts', qlb.astype(jnp.float32), ckv)
             + jnp.einsum('bthr,bsr->bhts', qrb.astype(jnp.float32), kr)) * scale
        q_pos = start + jnp.arange(Q_BLOCK)
        s = jnp.where(key_pos[None, :] <= q_pos[:, None], s, -jnp.inf)
        p = jax.nn.softmax(s, axis=-1)
        return jnp.einsum('bhts,bsc->bthc', p, ckv)

    o = lax.map(block, (ql_all, qr_all, jnp.arange(nb) * Q_BLOCK))
    return o.swapaxes(0, 1).reshape(B, S, MLA_HEADS, KV_LORA)


def _merge(state, s, vals, spec):
    m, l, acc = state
    m_new = jnp.maximum(m, s.max(axis=-1))
    corr = jnp.exp(m - m_new)
    p = jnp.exp(s - m_new[..., None])
    return (m_new, l * corr + p.sum(axis=-1),
            acc * corr[..., None] + jnp.einsum(spec, p, vals.astype(jnp.float32)))


def _fox_sample(q, k, v, logf, cache_k, cache_v, cache_logf, page_table, layer):
    Bd, T = q.shape[:2]
    J, G = FOX_KV_HEADS, FOX_GROUP
    qg = q.reshape(Bd, T, J, G, HEAD_DIM).astype(jnp.float32) * (HEAD_DIM ** -0.5)
    gcum = jnp.cumsum(logf, axis=1).reshape(Bd, T, J, G).transpose(0, 2, 3, 1)
    spec = 'bjgts,bsjd->bjgtd'
    s = jnp.einsum('btjgd,bsjd->bjgts', qg, k.astype(jnp.float32)) + gcum[..., :, None] - gcum[..., None, :]
    s = jnp.where(jnp.tril(jnp.ones((T, T), bool)), s, -jnp.inf)
    init = (jnp.full((Bd, J, G, T), -jnp.inf, jnp.float32), jnp.zeros((Bd, J, G, T), jnp.float32),
            jnp.zeros((Bd, J, G, T, HEAD_DIM), jnp.float32))
    state = _merge(init, s, v, spec)

    def page_step(carry, phys):
        st, suffix = carry
        kp, vp = cache_k[layer, phys], cache_v[layer, phys]
        lf = cache_logf[layer, phys].astype(jnp.float32)
        lf = lf.reshape(Bd, lf.shape[1], J, G)
        r = lax.cumsum(lf, axis=1, reverse=True) - lf + suffix[:, None]
        sp = (jnp.einsum('btjgd,bsjd->bjgts', qg, kp.astype(jnp.float32))
              + gcum[..., :, None] + r.transpose(0, 2, 3, 1)[..., None, :])
        return (_merge(st, sp, vp, spec), suffix + lf.sum(axis=1)), None

    (state, _), _ = lax.scan(page_step, (state, jnp.zeros((Bd, J, G), jnp.float32)), page_table.T[::-1])
    _, l, acc = state
    o = acc / l[..., None]
    return o.transpose(0, 3, 1, 2, 4).reshape(Bd, T, FOX_HEADS, HEAD_DIM)


def _mla_sample(q_lat, q_rope, c_kv, k_rope, cache_ckv, cache_kr, page_table, layer):
    Bd, T = q_lat.shape[:2]
    scale = (MLA_NOPE + MLA_ROPE) ** -0.5
    ql = q_lat.astype(jnp.float32) * scale
    qr = q_rope.astype(jnp.float32) * scale
    spec = 'bhts,bsc->bhtc'

    def scores(ckv, kr):
        return (jnp.einsum('bthc,bsc->bhts', ql, ckv.astype(jnp.float32))
                + jnp.einsum('bthr,bsr->bhts', qr, kr.astype(jnp.float32)))

    s = jnp.where(jnp.tril(jnp.ones((T, T), bool)), scores(c_kv, k_rope), -jnp.inf)
    init = (jnp.full((Bd, MLA_HEADS, T), -jnp.inf, jnp.float32), jnp.zeros((Bd, MLA_HEADS, T), jnp.float32),
            jnp.zeros((Bd, MLA_HEADS, T, KV_LORA), jnp.float32))
    state = _merge(init, s, c_kv, spec)

    def page_step(st, phys):
        ckv, kr = cache_ckv[layer, phys], cache_kr[layer, phys]
        return _merge(st, scores(ckv, kr), ckv, spec), None

    state, _ = lax.scan(page_step, state, page_table.T)
    _, l, acc = state
    return (acc / l[..., None]).transpose(0, 2, 1, 3)


def _peer(h, w_q, sub_keys, expert_u, expert_v):
    lead = h.shape[:-1]
    xt = h.reshape(-1, D_MODEL)
    n_tok = xt.shape[0]
    pad = (-n_tok) % TOK_BLOCK
    blocks = jnp.pad(xt, ((0, pad), (0, 0))).reshape(-1, TOK_BLOCK, D_MODEL)
    kk = PEER_TOPK * PEER_TOPK

    def block(xb):
        q = (xb @ w_q).reshape(TOK_BLOCK, PEER_HEADS, 2, PEER_HALF)
        s = jnp.einsum('thpd,hpnd->thpn', q, sub_keys).astype(jnp.float32)
        sv, si = lax.top_k(s, PEER_TOPK)
        cand_s = (sv[:, :, 0, :, None] + sv[:, :, 1, None, :]).reshape(TOK_BLOCK, PEER_HEADS, kk)
        cand_i = (si[:, :, 0, :, None] * N_KEYS + si[:, :, 1, None, :]).reshape(TOK_BLOCK, PEER_HEADS, kk)
        top_s, top_j = lax.top_k(cand_s, PEER_TOPK)
        e_idx = jnp.take_along_axis(cand_i, top_j, axis=-1)
        g = jax.nn.softmax(top_s, axis=-1)
        act = jax.nn.gelu(jnp.einsum('td,thkd->thk', xb, expert_u[e_idx]).astype(jnp.float32), approximate=False)
        return jnp.einsum('thk,thkd->td', g * act, expert_v[e_idx].astype(jnp.float32))

    y = lax.map(block, blocks).reshape(-1, D_MODEL)[:n_tok]
    return y.reshape(*lead, D_MODEL)


def _layer(x, c, pos, attend, w_ada, b_ada, g_pre_mix, g_post_mix, g_pre_ffn, g_post_ffn, w_in, b_forget,
           g_cq, w_uq, g_ckv, w_uk, w_uv, w_o, w_peer_q, peer_subkeys, peer_u, peer_v):
    dt = x.dtype
    shift1, scale1, gate1, shift2, scale2, gate2 = _adaln(c, w_ada, b_ada)
    h = _rmsnorm(x, g_pre_mix) * (1.0 + scale1) + shift1
    q_f, k_f, v_f, logf, q_lat, q_rope, c_kv, k_rope, gate_logit = _mixer_in(h, pos, w_in, b_forget, g_cq, w_uq, g_ckv, w_uk)
    o_fox, o_lat = attend(q_f, k_f, v_f, logf, q_lat, q_rope, c_kv, k_rope)
    x = x + gate1 * _rmsnorm(_mixer_out(o_fox, o_lat, gate_logit, w_uv, w_o), g_post_mix)
    h2 = _rmsnorm(x, g_pre_ffn) * (1.0 + scale2) + shift2
    x = x + gate2 * _rmsnorm(_peer(h2, w_peer_q, peer_subkeys, peer_u, peer_v), g_post_ffn)
    return x.astype(dt), (k_f, v_f, logf, c_kv, k_rope)


def setup_inputs(seed: int = 0) -> dict:
    key = jax.random.key(seed)
    ks = jax.random.split(key, 32)
    n_pages = PAST_LEN // PAGE_SIZE
    n_used = DEC_BATCH * n_pages
    n_pool = n_used + max(1, n_used // CACHE_SPARE_DIV)
    in_width = sum(_in_sizes())

    def nrm(k, shape, s=1.0):
        return jax.random.normal(k, shape, jnp.float32) * s

    def gain(k, n):
        return 1.0 + nrm(k, (DEPTH, n), 0.05)

    page_table = jax.random.permutation(ks[9], n_pool)[:n_used].reshape(DEC_BATCH, n_pages).astype(jnp.int32)
    return {
        'x_prompt': nrm(ks[0], (BATCH, SEQ, D_MODEL)),
        'x_sample': nrm(ks[1], (DEC_BATCH, DEC_SEQ, D_MODEL)),
        'c_prompt': nrm(ks[2], (BATCH, D_MODEL)),
        'c_sample': nrm(ks[3], (DEC_BATCH, D_MODEL)),
        'cache_fox_k': nrm(ks[4], (DEPTH, n_pool, PAGE_SIZE, FOX_KV_HEADS, HEAD_DIM)),
        'cache_fox_v': nrm(ks[5], (DEPTH, n_pool, PAGE_SIZE, FOX_KV_HEADS, HEAD_DIM)),
        'cache_fox_logf': jax.nn.log_sigmoid(FORGET_BIAS_INIT + nrm(ks[6], (DEPTH, n_pool, PAGE_SIZE, FOX_HEADS))),
        'cache_mla_ckv': nrm(ks[7], (DEPTH, n_pool, PAGE_SIZE, KV_LORA)),
        'cache_mla_krope': nrm(ks[8], (DEPTH, n_pool, PAGE_SIZE, MLA_ROPE)),
        'page_table': page_table,
        'w_ada': nrm(ks[10], (DEPTH, D_MODEL, 6 * D_MODEL), 0.5 * D_MODEL ** -0.5),
        'b_ada': nrm(ks[11], (DEPTH, 6 * D_MODEL), 0.01),
        'g_pre_mix': gain(ks[12], D_MODEL),
        'g_post_mix': gain(ks[13], D_MODEL),
        'g_pre_ffn': gain(ks[14], D_MODEL),
        'g_post_ffn': gain(ks[15], D_MODEL),
        'w_in': nrm(ks[16], (DEPTH, D_MODEL, in_width), D_MODEL ** -0.5),
        'b_forget': FORGET_BIAS_INIT + nrm(ks[17], (DEPTH, FOX_HEADS), 0.5),
        'g_cq': gain(ks[18], Q_LORA),
        'w_uq': nrm(ks[19], (DEPTH, Q_LORA, MLA_HEADS * (MLA_NOPE + MLA_ROPE)), Q_LORA ** -0.5),
        'g_ckv': gain(ks[20], KV_LORA),
        'w_uk': nrm(ks[21], (DEPTH, KV_LORA, MLA_HEADS, MLA_NOPE), KV_LORA ** -0.5),
        'w_uv': nrm(ks[22], (DEPTH, KV_LORA, MLA_HEADS, MLA_V), KV_LORA ** -0.5),
        'w_o': nrm(ks[23], (DEPTH, MIX_WIDTH, D_MODEL), MIX_WIDTH ** -0.5),
        'w_peer_q': nrm(ks[24], (DEPTH, D_MODEL, PEER_HEADS * PEER_QDIM), D_MODEL ** -0.5),
        'peer_subkeys': nrm(ks[25], (DEPTH, PEER_HEADS, 2, N_KEYS, PEER_HALF), PEER_HALF ** -0.5),
        'peer_u': nrm(ks[26], (DEPTH, N_EXPERTS, D_MODEL), D_MODEL ** -0.5),
        'peer_v': nrm(ks[27], (DEPTH, N_EXPERTS, D_MODEL), PEER_HEADS ** -0.5),
    }


def reference(x_prompt, x_sample, c_prompt, c_sample, cache_fox_k, cache_fox_v, cache_fox_logf,
              cache_mla_ckv, cache_mla_krope, page_table, w_ada, b_ada, g_pre_mix, g_post_mix,
              g_pre_ffn, g_post_ffn, w_in, b_forget, g_cq, w_uq, g_ckv, w_uk, w_uv, w_o,
              w_peer_q, peer_subkeys, peer_u, peer_v):
    pos_prompt = jnp.arange(x_prompt.shape[1])
    pos_sample = PAST_LEN + jnp.arange(x_sample.shape[1])
    y_prompt, y_sample = x_prompt, x_sample
    rows_prompt, rows_sample = [], []
    for layer in range(DEPTH):
        lw = (w_ada[layer], b_ada[layer], g_pre_mix[layer], g_post_mix[layer], g_pre_ffn[layer],
              g_post_ffn[layer], w_in[layer], b_forget[layer], g_cq[layer], w_uq[layer], g_ckv[layer],
              w_uk[layer], w_uv[layer], w_o[layer], w_peer_q[layer], peer_subkeys[layer],
              peer_u[layer], peer_v[layer])

        def attend_prompt(q_f, k_f, v_f, logf, q_lat, q_rope, c_kv, k_rope):
            return _fox_prompt(q_f, k_f, v_f, logf), _mla_prompt(q_lat, q_rope, c_kv, k_rope)

        def attend_sample(q_f, k_f, v_f, logf, q_lat, q_rope, c_kv, k_rope, layer=layer):
            return (_fox_sample(q_f, k_f, v_f, logf, cache_fox_k, cache_fox_v, cache_fox_logf, page_table, layer),
                    _mla_sample(q_lat, q_rope, c_kv, k_rope, cache_mla_ckv, cache_mla_krope, page_table, layer))

        y_prompt, rp = _layer(y_prompt, c_prompt, pos_prompt, attend_prompt, *lw)
        y_sample, rs = _layer(y_sample, c_sample, pos_sample, attend_sample, *lw)
        rows_prompt.append(rp)
        rows_sample.append(rs)
    pk, pv, pl, pc, pr = [jnp.stack(a) for a in zip(*rows_prompt)]
    sk, sv, sl, sc, sr = [jnp.stack(a) for a in zip(*rows_sample)]
    return (y_prompt, y_sample, pk, pv, pl, pc, pr, sk, sv, sl, sc, sr)
```

```python
import functools

import numpy as np
import jax
import jax.numpy as jnp
from jax import lax
from jax.experimental import pallas as pl
from jax.experimental.pallas import tpu as pltpu

D_MODEL = 2048
PAST_LEN = 8192
PAGE_SIZE = 128
FOX_HEADS = 8
FOX_KV_HEADS = 4
FOX_GROUP = FOX_HEADS // FOX_KV_HEADS
HEAD_DIM = 128
MLA_HEADS = 8
MLA_NOPE = 128
MLA_ROPE = 64
MLA_V = 128
Q_LORA = 512
KV_LORA = 512
ROPE_THETA = 10000.0
PEER_HEADS = 8
N_KEYS = 128
PEER_TOPK = 16
PEER_HALF = 128
RMS_EPS = 1e-6

FOX_SCALE = HEAD_DIM ** -0.5
MLA_SCALE = (MLA_NOPE + MLA_ROPE) ** -0.5
MLA_QK = KV_LORA + 128
NEG_BIG = -1e30

BF16 = jnp.bfloat16
F32 = jnp.float32

VMEM_LIMIT = 56 * 1024 * 1024


def _cparams(sem, vmem=VMEM_LIMIT):
    return pltpu.CompilerParams(dimension_semantics=sem, vmem_limit_bytes=vmem)


def _const_spec(shape):
    nd = len(shape)
    return pl.BlockSpec(shape, lambda *_: (0,) * nd, pipeline_mode=pl.Buffered(1))


def _rms(x, g):
    return x * lax.rsqrt(jnp.mean(x * x, axis=-1, keepdims=True) + RMS_EPS) * g


def _dot(a, b):
    return jnp.dot(a, b, preferred_element_type=F32)


def _dot_t(a, b):
    return lax.dot_general(a, b, (((1,), (1,)), ((), ())), preferred_element_type=F32)


def _ada_kernel(c_ref, w_ref, b_ref, o_ref):
    c = c_ref[...]
    a = (c * jax.nn.sigmoid(c)).astype(BF16)
    o_ref[...] = _dot(a, w_ref[...].astype(BF16)) + b_ref[...]


def _adaln(c, w_ada, b_ada, tn=1024):
    rows, d = c.shape
    n = w_ada.shape[1]
    return pl.pallas_call(
        _ada_kernel,
        grid=(n // tn,),
        in_specs=[pl.BlockSpec((rows, d), lambda j: (0, 0)),
                  pl.BlockSpec((d, tn), lambda j: (0, j)),
                  pl.BlockSpec((1, tn), lambda j: (0, j))],
        out_specs=pl.BlockSpec((rows, tn), lambda j: (0, j)),
        out_shape=jax.ShapeDtypeStruct((rows, n), F32),
        compiler_params=_cparams(("arbitrary",)),
        name="adaln",
    )(c, w_ada, b_ada.reshape(1, n))


def _in_a_kernel(x_ref, shift_ref, scale_ref, g_ref, w_ref,
                 hb_ref, qf_ref, kf_ref, vf_ref, kb_ref, vb_ref):
    x = x_ref[...]
    h = _rms(x, g_ref[...]) * (1.0 + scale_ref[...]) + shift_ref[...]
    hb = h.astype(BF16)
    hb_ref[...] = hb
    qkv = _dot(hb, w_ref[...])
    nq = qf_ref.shape[1]
    nk = kf_ref.shape[1]
    qf_ref[...] = (qkv[:, :nq] * FOX_SCALE).astype(BF16)
    kf = qkv[:, nq:nq + nk]
    vf = qkv[:, nq + nk:]
    kf_ref[...] = kf
    vf_ref[...] = vf
    kb_ref[...] = kf.astype(BF16)
    vb_ref[...] = vf.astype(BF16)


def _in_b_kernel(hb_ref, w_ref, gate_ref):
    gate_ref[...] = jax.nn.sigmoid(_dot(hb_ref[...], w_ref[...]))


def _in_c_kernel(hb_ref, w_ref, bf_ref, gcq_ref, gckv_ref, wuq_ref, wuk_ref, cos_ref, sin_ref,
                 logf_ref, ckv_ref, kr_ref, kcat_ref, qcat_ref):
    sm = _dot(hb_ref[...], w_ref[...])
    cq = sm[:, :Q_LORA]
    ckv = sm[:, Q_LORA:Q_LORA + KV_LORA]
    o = Q_LORA + KV_LORA
    kr = sm[:, o:o + 128]
    krr = sm[:, o + 128:o + 256]
    fl = sm[:, o + 256:o + 256 + FOX_HEADS]
    logf_ref[...] = jax.nn.log_sigmoid(fl + bf_ref[...])
    c_kv = _rms(ckv, gckv_ref[...])
    ckv_ref[...] = c_kv
    cos = cos_ref[...]
    sin = sin_ref[...]
    k_rope = kr * cos + krr * sin
    kr_ref[...] = k_rope[:, :MLA_ROPE]
    kcat_ref[:, :KV_LORA] = c_kv.astype(BF16)
    kcat_ref[:, KV_LORA:] = k_rope.astype(BF16)
    cqn = _rms(cq, gcq_ref[...]).astype(BF16)
    qm = _dot(cqn, wuq_ref[...])
    hn = MLA_HEADS * MLA_NOPE
    for h in range(MLA_HEADS):
        qn = qm[:, h * MLA_NOPE:(h + 1) * MLA_NOPE].astype(BF16)
        ql = _dot(qn, wuk_ref[h])
        qcat_ref[h, :, :KV_LORA] = (ql * MLA_SCALE).astype(BF16)
        qr = qm[:, hn + h * 128:hn + (h + 1) * 128]
        qrr = qm[:, 2 * hn + h * 128:2 * hn + (h + 1) * 128]
        qcat_ref[h, :, KV_LORA:] = ((qr * cos + qrr * sin) * MLA_SCALE).astype(BF16)


def _mod_spec(per_row, tm, chunk, rows_per_mod):
    if per_row:
        return pl.BlockSpec((tm, D_MODEL), lambda i: (i, chunk))
    return pl.BlockSpec((None, 1, D_MODEL), lambda i: (i * tm // rows_per_mod, 0, chunk))


def _mixer_in(x, mod, per_row, rows_per_mod, cos, sin, pos_blocks, wts, tm):
    n = x.shape[0]
    grid = (n // tm,)
    row = lambda w: pl.BlockSpec((tm, w), lambda i: (i, 0))
    sds = lambda w, dt: jax.ShapeDtypeStruct((n, w), dt)
    nqf = FOX_HEADS * HEAD_DIM
    nkf = FOX_KV_HEADS * HEAD_DIM
    hb, qf, kf, vf, kb, vb = pl.pallas_call(
        _in_a_kernel,
        grid=grid,
        in_specs=[row(D_MODEL), _mod_spec(per_row, tm, 0, rows_per_mod), _mod_spec(per_row, tm, 1, rows_per_mod),
                  _const_spec((1, D_MODEL)), _const_spec(wts["w_qkv"].shape)],
        out_specs=[row(D_MODEL), row(nqf), row(nkf), row(nkf), row(nkf), row(nkf)],
        out_shape=[sds(D_MODEL, BF16), sds(nqf, BF16), sds(nkf, F32), sds(nkf, F32), sds(nkf, BF16), sds(nkf, BF16)],
        compiler_params=_cparams(("arbitrary",)),
        name="mixer_in_a",
    )(x, mod, mod, wts["g_pre_mix"], wts["w_qkv"])
    gate = pl.pallas_call(
        _in_b_kernel,
        grid=grid,
        in_specs=[row(D_MODEL), _const_spec(wts["w_gate"].shape)],
        out_specs=row(D_MODEL),
        out_shape=sds(D_MODEL, F32),
        compiler_params=_cparams(("arbitrary",)),
        name="mixer_in_b",
    )(hb, wts["w_gate"])
    pos_spec = pl.BlockSpec((tm, 128), lambda i: (i % pos_blocks, 0))
    logf, c_kv, k_rope, kcat, qcat = pl.pallas_call(
        _in_c_kernel,
        grid=grid,
        in_specs=[row(D_MODEL), _const_spec(wts["w_sm"].shape), _const_spec((1, FOX_HEADS)),
                  _const_spec((1, Q_LORA)), _const_spec((1, KV_LORA)), _const_spec(wts["w_uq"].shape),
                  _const_spec(wts["w_uk"].shape), pos_spec, pos_spec],
        out_specs=[row(FOX_HEADS), row(KV_LORA), row(MLA_ROPE), row(MLA_QK),
                   pl.BlockSpec((MLA_HEADS, tm, MLA_QK), lambda i: (0, i, 0))],
        out_shape=[sds(FOX_HEADS, F32), sds(KV_LORA, F32), sds(MLA_ROPE, F32), sds(MLA_QK, BF16),
                   jax.ShapeDtypeStruct((MLA_HEADS, n, MLA_QK), BF16)],
        compiler_params=_cparams(("arbitrary",)),
        name="mixer_in_c",
    )(hb, wts["w_sm"], wts["b_forget"], wts["g_cq"], wts["g_ckv"], wts["w_uq"], wts["w_uk"], cos, sin)
    return dict(qf=qf, kf=kf, vf=vf, kb=kb, vb=vb, gate=gate, logf=logf, c_kv=c_kv, k_rope=k_rope,
                kcat=kcat, qcat=qcat)


def _fox_prompt_kernel(q_ref, k_ref, v_ref, fcol_ref, frow_ref, o_ref, *, tq):
    qi = pl.program_id(2)
    q = jnp.concatenate([q_ref[:, :HEAD_DIM], q_ref[:, HEAD_DIM:]], axis=0)
    fq = [fcol_ref[:, g:g + 1] for g in range(FOX_GROUP)]

    def step(kb, carry, masked):
        start = pl.multiple_of(kb * tq, tq)
        k = k_ref[pl.ds(start, tq), :]
        v = v_ref[pl.ds(start, tq), :]
        s = _dot_t(q, k)
        ps, new = [], []
        for g in range(FOX_GROUP):
            m, l = carry[2 * g], carry[2 * g + 1]
            sg = s[g * tq:(g + 1) * tq] + fq[g] - frow_ref[g:g + 1, pl.ds(start, tq)]
            if masked:
                r = lax.broadcasted_iota(jnp.int32, (tq, tq), 0)
                c = lax.broadcasted_iota(jnp.int32, (tq, tq), 1)
                sg = jnp.where(c <= r, sg, NEG_BIG)
            m_new = jnp.maximum(m, jnp.max(sg, axis=1, keepdims=True))
            corr = jnp.exp(m - m_new)
            p = jnp.exp(sg - m_new)
            new += [m_new, l * corr + jnp.sum(p, axis=1, keepdims=True), corr]
            ps.append(p.astype(BF16))
        pv = _dot(jnp.concatenate(ps, axis=0), v)
        acc = carry[4]
        corr = jnp.concatenate([jnp.broadcast_to(new[2], (tq, HEAD_DIM)), jnp.broadcast_to(new[5], (tq, HEAD_DIM))], axis=0)
        return (new[0], new[1], new[3], new[4], acc * corr + pv)

    init = (jnp.full((tq, 1), NEG_BIG, F32), jnp.zeros((tq, 1), F32),
            jnp.full((tq, 1), NEG_BIG, F32), jnp.zeros((tq, 1), F32),
            jnp.zeros((FOX_GROUP * tq, HEAD_DIM), F32))
    carry = lax.fori_loop(0, qi, lambda kb, c: step(kb, c, False), init)
    m0, l0, m1, l1, acc = step(qi, carry, True)
    o_ref[:, :HEAD_DIM] = acc[:tq] / l0
    o_ref[:, HEAD_DIM:] = acc[tq:] / l1


def _fox_prompt(qf, kb, vb, fcol, frow, batch, seq, tq=256):
    n = qf.shape[0]
    nq = seq // tq
    gw = FOX_GROUP * HEAD_DIM
    return pl.pallas_call(
        functools.partial(_fox_prompt_kernel, tq=tq),
        grid=(batch, FOX_KV_HEADS, nq),
        in_specs=[pl.BlockSpec((tq, gw), lambda b, j, i: (b * nq + i, j)),
                  pl.BlockSpec((seq, HEAD_DIM), lambda b, j, i: (b, j)),
                  pl.BlockSpec((seq, HEAD_DIM), lambda b, j, i: (b, j)),
                  pl.BlockSpec((None, tq, FOX_GROUP), lambda b, j, i: (j, b * nq + i, 0)),
                  pl.BlockSpec((None, None, FOX_GROUP, seq), lambda b, j, i: (b, j, 0, 0))],
        out_specs=pl.BlockSpec((tq, gw), lambda b, j, i: (b * nq + i, j)),
        out_shape=jax.ShapeDtypeStruct((n, FOX_HEADS * HEAD_DIM), F32),
        compiler_params=_cparams(("arbitrary", "arbitrary", "arbitrary")),
        name="fox_prompt",
    )(qf, kb, vb, fcol, frow)


def _mla_prompt_kernel(q_ref, k_ref, wuv_ref, o_ref, m_ref, l_ref, acc_ref, *, tq, tk):
    qi = pl.program_id(1)
    rows = MLA_HEADS * tq
    q = q_ref[...].reshape(rows, MLA_QK)
    m_ref[...] = jnp.full(m_ref.shape, NEG_BIG, F32)
    l_ref[...] = jnp.zeros(l_ref.shape, F32)
    acc_ref[...] = jnp.zeros(acc_ref.shape, F32)

    def step(kb, masked):
        start = pl.multiple_of(kb * tk, tk)
        k = k_ref[pl.ds(start, tk), :]
        s = _dot_t(q, k)
        if masked:
            r = lax.broadcasted_iota(jnp.int32, (rows, tk), 0)
            c = lax.broadcasted_iota(jnp.int32, (rows, tk), 1)
            qpos = qi * tq + (r & (tq - 1))
            s = jnp.where(start + c <= qpos, s, NEG_BIG)
        m = m_ref[...]
        m_new = jnp.maximum(m, jnp.max(s, axis=1, keepdims=True))
        corr = jnp.exp(m - m_new)
        p = jnp.exp(s - m_new)
        l_ref[...] = l_ref[...] * corr + jnp.sum(p, axis=1, keepdims=True)
        m_ref[...] = m_new
        acc_ref[...] = acc_ref[...] * corr + _dot(p.astype(BF16), k[:, :KV_LORA])

    last = (qi * tq) // tk

    def body(kb, _):
        step(kb, False)
        return 0

    lax.fori_loop(0, last, body, 0)
    step(last, True)
    o = acc_ref[...] / l_ref[...]
    for h in range(MLA_HEADS):
        oh = o[h * tq:(h + 1) * tq].astype(BF16)
        o_ref[:, h * MLA_V:(h + 1) * MLA_V] = _dot(oh, wuv_ref[h])


def _mla_prompt(qcat, kcat, wuv, batch, seq, tq=128, tk=256):
    n = kcat.shape[0]
    nq = seq // tq
    rows = MLA_HEADS * tq
    return pl.pallas_call(
        functools.partial(_mla_prompt_kernel, tq=tq, tk=tk),
        grid=(batch, nq),
        in_specs=[pl.BlockSpec((MLA_HEADS, tq, MLA_QK), lambda b, i: (0, b * nq + i, 0)),
                  pl.BlockSpec((seq, MLA_QK), lambda b, i: (b, 0)),
                  _const_spec(wuv.shape)],
        out_specs=pl.BlockSpec((tq, MLA_HEADS * MLA_V), lambda b, i: (b * nq + i, 0)),
        out_shape=jax.ShapeDtypeStruct((n, MLA_HEADS * MLA_V), F32),
        scratch_shapes=[pltpu.VMEM((rows, 1), F32), pltpu.VMEM((rows, 1), F32), pltpu.VMEM((rows, KV_LORA), F32)],
        compiler_params=_cparams(("arbitrary", "arbitrary")),
        name="mla_prompt",
    )(qcat, kcat, wuv)


def _strict_upper(n):
    return jnp.asarray(np.tril(np.ones((n, n), np.float32), -1))


def _fox_sample_kernel(pt_ref, q_ref, kn_ref, vn_ref, lfn_ref, up_ref, *refs, pages):
    k_refs = refs[:pages]
    v_refs = refs[pages:2 * pages]
    lf_refs = refs[2 * pages:3 * pages]
    o_ref = refs[3 * pages]
    m_ref, l_ref, acc_ref, suf_ref = refs[3 * pages + 1:]
    g = pl.program_id(1)
    J, R = FOX_KV_HEADS, 16
    T = kn_ref.shape[0]

    lfn = lfn_ref[...]
    gc = [lfn[:, 0:1]]
    for s in range(1, T):
        gc.append(gc[-1] + lfn[:, s:s + 1])
    trow = lax.broadcasted_iota(jnp.int32, (J * R, 1), 0) & (T - 1)
    grow = gc[T - 1]
    for t in range(T - 2, -1, -1):
        grow = jnp.where(trow == t, gc[t], grow)

    @pl.when(g == 0)
    def _():
        suf_ref[...] = jnp.zeros(suf_ref.shape, F32)
        for j in range(J):
            qj = q_ref[j].astype(F32)
            gr = grow[j * R:(j + 1) * R]
            ss = []
            for s in range(T):
                ks = kn_ref[s:s + 1, j * HEAD_DIM:(j + 1) * HEAD_DIM]
                sc = jnp.sum(qj * ks, axis=1, keepdims=True) + gr - gc[s][j * R:(j + 1) * R]
                ss.append(jnp.where(trow[j * R:(j + 1) * R] >= s, sc, NEG_BIG))
            m = ss[0]
            for s in range(1, T):
                m = jnp.maximum(m, ss[s])
            l = jnp.zeros((R, 1), F32)
            acc = jnp.zeros((R, HEAD_DIM), F32)
            for s in range(T):
                p = jnp.exp(ss[s] - m)
                l = l + p
                acc = acc + p * vn_ref[s:s + 1, j * HEAD_DIM:(j + 1) * HEAD_DIM]
            m_ref[j] = m
            l_ref[j] = l
            acc_ref[j] = acc

    upper = up_ref[...]
    rall = lax.broadcasted_iota(jnp.int32, (J * R, 1), 0)
    rin = rall & (R - 1)
    rowhead = jnp.where(rin < FOX_GROUP * T, (rall // R) * FOX_GROUP + rin // T, -1)
    for r in range(pages):
        lft = lf_refs[r][...]
        suffix = suf_ref[...]
        rt = jnp.dot(lft, upper, preferred_element_type=F32, precision=lax.Precision.HIGHEST) + suffix
        suf_ref[...] = suffix + jnp.sum(lft, axis=1, keepdims=True)
        bias = jnp.broadcast_to(grow, (J * R, PAGE_SIZE))
        for h in range(FOX_HEADS):
            bias = bias + jnp.where(rowhead == h, rt[h:h + 1, :], 0.0)
        for j in range(J):
            kp = k_refs[r][:, j * HEAD_DIM:(j + 1) * HEAD_DIM].astype(BF16)
            vp = v_refs[r][:, j * HEAD_DIM:(j + 1) * HEAD_DIM].astype(BF16)
            s = _dot_t(q_ref[j], kp) + bias[j * R:(j + 1) * R]
            m = m_ref[j]
            m_new = jnp.maximum(m, jnp.max(s, axis=1, keepdims=True))
            corr = jnp.exp(m - m_new)
            p = jnp.exp(s - m_new)
            l_ref[j] = l_ref[j] * corr + jnp.sum(p, axis=1, keepdims=True)
            m_ref[j] = m_new
            acc_ref[j] = acc_ref[j] * corr + _dot(p.astype(BF16), vp)

    @pl.when(g == pl.num_programs(1) - 1)
    def _():
        for j in range(J):
            o_ref[j] = acc_ref[j] / l_ref[j]


def _fox_sample(pt_flat, q, kn, vn, lfn, cache_k, cache_v, cache_lft, n_pages, pages=8):
    bd = q.shape[0]
    J, R = FOX_KV_HEADS, 16
    T = kn.shape[1]
    ng = n_pages // pages
    upper = _strict_upper(PAGE_SIZE)

    def page_map(r):
        return lambda b, g, pt: (pt[b * n_pages + (n_pages - 1 - (g * pages + r))], 0, 0)

    kw = FOX_KV_HEADS * HEAD_DIM
    in_specs = [pl.BlockSpec((None, J, R, HEAD_DIM), lambda b, g, pt: (b, 0, 0, 0)),
                pl.BlockSpec((None, T, kw), lambda b, g, pt: (b, 0, 0)),
                pl.BlockSpec((None, T, kw), lambda b, g, pt: (b, 0, 0)),
                pl.BlockSpec((None, J * R, T), lambda b, g, pt: (b, 0, 0)),
                pl.BlockSpec((PAGE_SIZE, PAGE_SIZE), lambda b, g, pt: (0, 0))]
    in_specs += [pl.BlockSpec((None, PAGE_SIZE, kw), page_map(r)) for r in range(pages)]
    in_specs += [pl.BlockSpec((None, PAGE_SIZE, kw), page_map(r)) for r in range(pages)]
    in_specs += [pl.BlockSpec((None, FOX_HEADS, PAGE_SIZE), page_map(r)) for r in range(pages)]
    grid_spec = pltpu.PrefetchScalarGridSpec(
        num_scalar_prefetch=1,
        grid=(bd, ng),
        in_specs=in_specs,
        out_specs=pl.BlockSpec((None, J, R, HEAD_DIM), lambda b, g, pt: (b, 0, 0, 0)),
        scratch_shapes=[pltpu.VMEM((J, R, 1), F32), pltpu.VMEM((J, R, 1), F32), pltpu.VMEM((J, R, HEAD_DIM), F32),
                        pltpu.VMEM((FOX_HEADS, 1), F32)],
    )
    return pl.pallas_call(
        functools.partial(_fox_sample_kernel, pages=pages),
        grid_spec=grid_spec,
        out_shape=jax.ShapeDtypeStruct((bd, J, R, HEAD_DIM), F32),
        compiler_params=_cparams(("arbitrary", "arbitrary")),
        name="fox_sample",
    )(pt_flat, q, kn, vn, lfn, upper, *([cache_k] * pages), *([cache_v] * pages), *([cache_lft] * pages))


def _mla_sample_kernel(pt_ref, q_ref, kn_ref, *refs, pages):
    c_refs = refs[:pages]
    r_refs = refs[pages:2 * pages]
    o_ref = refs[2 * pages]
    m_ref, l_ref, acc_ref = refs[2 * pages + 1:]
    g = pl.program_id(1)
    T = kn_ref.shape[0]
    rows = q_ref.shape[0]
    q = q_ref[...]

    @pl.when(g == 0)
    def _():
        qf = q.astype(F32)
        trow = lax.broadcasted_iota(jnp.int32, (rows, 1), 0) & (T - 1)
        ss = []
        for s in range(T):
            sc = jnp.sum(qf * kn_ref[s:s + 1, :], axis=1, keepdims=True)
            ss.append(jnp.where(trow >= s, sc, NEG_BIG))
        m = ss[0]
        for s in range(1, T):
            m = jnp.maximum(m, ss[s])
        l = jnp.zeros((rows, 1), F32)
        acc = jnp.zeros((rows, KV_LORA), F32)
        for s in range(T):
            p = jnp.exp(ss[s] - m)
            l = l + p
            acc = acc + p * kn_ref[s:s + 1, :KV_LORA]
        m_ref[...] = m
        l_ref[...] = l
        acc_ref[...] = acc

    ql = q[:, :KV_LORA]
    qr = q[:, KV_LORA:KV_LORA + MLA_ROPE]
    for r in range(pages):
        ckv = c_refs[r][...].astype(BF16)
        kr = r_refs[r][...].astype(BF16)
        s = _dot_t(ql, ckv) + _dot_t(qr, kr)
        m = m_ref[...]
        m_new = jnp.maximum(m, jnp.max(s, axis=1, keepdims=True))
        corr = jnp.exp(m - m_new)
        p = jnp.exp(s - m_new)
        l_ref[...] = l_ref[...] * corr + jnp.sum(p, axis=1, keepdims=True)
        m_ref[...] = m_new
        acc_ref[...] = acc_ref[...] * corr + _dot(p.astype(BF16), ckv)

    @pl.when(g == pl.num_programs(1) - 1)
    def _():
        o_ref[...] = acc_ref[...] / l_ref[...]


def _mla_sample(pt_flat, q, kn, cache_ckv, cache_kr, n_pages, pages=8):
    bd, rows, _ = q.shape
    T = kn.shape[1]
    ng = n_pages // pages

    def page_map(r):
        return lambda b, g, pt: (pt[b * n_pages + g * pages + r], 0, 0)

    in_specs = [pl.BlockSpec((None, rows, MLA_QK), lambda b, g, pt: (b, 0, 0)),
                pl.BlockSpec((None, T, MLA_QK), lambda b, g, pt: (b, 0, 0))]
    in_specs += [pl.BlockSpec((None, PAGE_SIZE, KV_LORA), page_map(r)) for r in range(pages)]
    in_specs += [pl.BlockSpec((None, PAGE_SIZE, MLA_ROPE), page_map(r)) for r in range(pages)]
    grid_spec = pltpu.PrefetchScalarGridSpec(
        num_scalar_prefetch=1,
        grid=(bd, ng),
        in_specs=in_specs,
        out_specs=pl.BlockSpec((None, rows, KV_LORA), lambda b, g, pt: (b, 0, 0)),
        scratch_shapes=[pltpu.VMEM((rows, 1), F32), pltpu.VMEM((rows, 1), F32), pltpu.VMEM((rows, KV_LORA), F32)],
    )
    return pl.pallas_call(
        functools.partial(_mla_sample_kernel, pages=pages),
        grid_spec=grid_spec,
        out_shape=jax.ShapeDtypeStruct((bd, rows, KV_LORA), F32),
        compiler_params=_cparams(("arbitrary", "arbitrary")),
        name="mla_sample",
    )(pt_flat, q, kn, *([cache_ckv] * pages), *([cache_kr] * pages))


def _uv_kernel(o_ref, w_ref, out_ref):
    for h in range(MLA_HEADS):
        out_ref[:, h * MLA_V:(h + 1) * MLA_V] = _dot(o_ref[h].astype(BF16), w_ref[h])


def _uv_proj(o_lat, wuv):
    _, n, c = o_lat.shape
    return pl.pallas_call(
        _uv_kernel,
        grid=(1,),
        in_specs=[pl.BlockSpec(o_lat.shape, lambda i: (0, 0, 0)), pl.BlockSpec(wuv.shape, lambda i: (0, 0, 0))],
        out_specs=pl.BlockSpec((n, MLA_HEADS * MLA_V), lambda i: (0, 0)),
        out_shape=jax.ShapeDtypeStruct((n, MLA_HEADS * MLA_V), F32),
        compiler_params=_cparams(("arbitrary",)),
        name="mla_uv",
    )(o_lat, wuv)


def _mixer_out_kernel(of_ref, om_ref, gate_ref, x_ref, g1_ref, sc2_ref, sh2_ref, gpost_ref, gpre_ref, wo_ref,
                      x1_ref, h2_ref):
    nf = of_ref.shape[1]
    mf = (of_ref[...] * gate_ref[:, :nf]).astype(BF16)
    mm = (om_ref[...] * gate_ref[:, nf:]).astype(BF16)
    y = _dot(mf, wo_ref[:nf, :]) + _dot(mm, wo_ref[nf:, :])
    x1 = x_ref[...] + g1_ref[...] * _rms(y, gpost_ref[...])
    x1_ref[...] = x1
    h2 = _rms(x1, gpre_ref[...]) * (1.0 + sc2_ref[...]) + sh2_ref[...]
    h2_ref[...] = h2.astype(BF16)


def _mixer_out(o_fox, o_mla, gate, x, mod, per_row, rows_per_mod, wts, tm):
    n = x.shape[0]
    row = lambda w: pl.BlockSpec((tm, w), lambda i: (i, 0))
    ms = lambda c: _mod_spec(per_row, tm, c, rows_per_mod)
    return pl.pallas_call(
        _mixer_out_kernel,
        grid=(n // tm,),
        in_specs=[row(o_fox.shape[1]), row(o_mla.shape[1]), row(D_MODEL), row(D_MODEL), ms(2), ms(4), ms(3),
                  _const_spec((1, D_MODEL)), _const_spec((1, D_MODEL)), _const_spec(wts["w_o"].shape)],
        out_specs=[row(D_MODEL), row(D_MODEL)],
        out_shape=[jax.ShapeDtypeStruct((n, D_MODEL), F32), jax.ShapeDtypeStruct((n, D_MODEL), BF16)],
        compiler_params=_cparams(("arbitrary",)),
        name="mixer_out",
    )(o_fox, o_mla, gate, x, mod, mod, mod, wts["g_post_mix"], wts["g_pre_ffn"], wts["w_o"])


_CAND_LIMITS = [PEER_TOPK // (a + 1) for a in range(PEER_TOPK)]


def _topk_rows(s, k):
    n = s.shape[0]
    row = lax.broadcasted_iota(jnp.int32, s.shape, 0).astype(F32)
    rank = jnp.full(s.shape, float(k), F32)
    work = s
    vals = []
    for i in range(k):
        m = jnp.max(work, axis=0, keepdims=True)
        idx = jnp.min(jnp.where(work == m, row, float(n)), axis=0, keepdims=True)
        hit = row == idx
        rank = jnp.where(hit, float(i), rank)
        work = jnp.where(hit, -jnp.inf, work)
        vals.append(m)
    return vals, rank


def _peer_route_kernel(ht_ref, wq_ref, sk_ref, a_ref, n_ref, b_ref, r2_ref):
    qt = _dot(wq_ref[...], ht_ref[...]).astype(BF16)
    t = qt.shape[1]
    for h in range(PEER_HEADS):
        s1 = _dot(sk_ref[h, 0], qt[(2 * h) * PEER_HALF:(2 * h + 1) * PEER_HALF])
        s2 = _dot(sk_ref[h, 1], qt[(2 * h + 1) * PEER_HALF:(2 * h + 2) * PEER_HALF])
        v1, r1 = _topk_rows(s1, PEER_TOPK)
        v2, r2 = _topk_rows(s2, PEER_TOPK)
        v2a = jnp.concatenate(v2[:8], axis=0)
        v2b = jnp.concatenate(v2[8:], axis=0)
        brow = lax.broadcasted_iota(jnp.int32, (8, t), 0)
        groups = [v1[0] + v2a, v1[0] + v2b]
        for a in range(1, PEER_TOPK):
            groups.append(jnp.where(brow < _CAND_LIMITS[a], v1[a] + v2a, -jnp.inf))
        cand = jnp.concatenate(groups, axis=0)
        cmax = groups[0][0:1]
        _, rc = _topk_rows(cand, PEER_TOPK)
        chosen = rc < float(PEER_TOPK)
        z = jnp.sum(jnp.where(chosen, jnp.exp(cand - cmax), 0.0), axis=0, keepdims=True)
        cnt = jnp.where(chosen, 1.0, 0.0)
        nkey = jnp.zeros(s1.shape, F32)
        for a in range(PEER_TOPK):
            lo = 0 if a == 0 else 8 * (a + 1)
            hi = 16 if a == 0 else lo + 8
            na = jnp.sum(cnt[lo:hi], axis=0, keepdims=True)
            nkey = nkey + jnp.where(r1 == float(a), na, 0.0)
        a_ref[h] = jnp.where(r1 < float(PEER_TOPK), jnp.exp(s1 - v1[0]), 0.0)
        n_ref[h] = nkey
        b_ref[h] = jnp.where(r2 < float(PEER_TOPK), jnp.exp(s2 - v2[0]), 0.0) / z
        r2_ref[h] = r2


def _peer_route(ht, wq_t, subkeys, tt=256):
    d, n = ht.shape
    shp = jax.ShapeDtypeStruct((PEER_HEADS, N_KEYS, n), F32)
    spec = pl.BlockSpec((PEER_HEADS, N_KEYS, tt), lambda i: (0, 0, i))
    return pl.pallas_call(
        _peer_route_kernel,
        grid=(n // tt,),
        in_specs=[pl.BlockSpec((d, tt), lambda i: (0, i)), _const_spec(wq_t.shape), _const_spec(subkeys.shape)],
        out_specs=[spec, spec, spec, spec],
        out_shape=[shp, shp, shp, shp],
        compiler_params=_cparams(("arbitrary",)),
        name="peer_route",
    )(ht, wq_t, subkeys)


def _peer_dense_kernel(ht_ref, u_ref, vt_ref, a_ref, n_ref, b_ref, r2_ref, y_ref, xu_ref, w_ref):
    eb = pl.program_id(1)

    @pl.when(eb == 0)
    def _():
        y_ref[...] = jnp.zeros(y_ref.shape, F32)

    xu_ref[...] = _dot(u_ref[...], ht_ref[...])
    ni = u_ref.shape[0] // N_KEYS
    for ii in range(ni):
        c = jnp.zeros((N_KEYS, ht_ref.shape[1]), F32)
        for h in range(PEER_HEADS):
            hit = r2_ref[h] < n_ref[h, ii:ii + 1, :]
            c = c + jnp.where(hit, b_ref[h], 0.0) * a_ref[h, ii:ii + 1, :]
        xu = xu_ref[ii * N_KEYS:(ii + 1) * N_KEYS, :]
        act = 0.5 * xu * (1.0 + lax.erf(xu * float(np.sqrt(0.5))))
        w_ref[ii * N_KEYS:(ii + 1) * N_KEYS, :] = (act * c).astype(BF16)
    y_ref[...] += _dot(vt_ref[...], w_ref[...])


def _peer_dense(ht, u, vt, a, nn, b, r2, tt=512, te=1024):
    d, n = ht.shape
    ne = u.shape[0]
    ni = te // N_KEYS
    return pl.pallas_call(
        _peer_dense_kernel,
        grid=(n // tt, ne // te),
        in_specs=[pl.BlockSpec((d, tt), lambda t, e: (0, t)),
                  pl.BlockSpec((te, d), lambda t, e: (e, 0)),
                  pl.BlockSpec((d, te), lambda t, e: (0, e)),
                  pl.BlockSpec((PEER_HEADS, ni, tt), lambda t, e: (0, e, t)),
                  pl.BlockSpec((PEER_HEADS, ni, tt), lambda t, e: (0, e, t)),
                  pl.BlockSpec((PEER_HEADS, N_KEYS, tt), lambda t, e: (0, 0, t)),
                  pl.BlockSpec((PEER_HEADS, N_KEYS, tt), lambda t, e: (0, 0, t))],
        out_specs=pl.BlockSpec((d, tt), lambda t, e: (0, t)),
        out_shape=jax.ShapeDtypeStruct((d, n), F32),
        scratch_shapes=[pltpu.VMEM((te, tt), F32), pltpu.VMEM((te, tt), BF16)],
        compiler_params=_cparams(("arbitrary", "arbitrary")),
        name="peer_dense",
    )(ht, u, vt, a, nn, b, r2)


def _final_kernel(x1_ref, y_ref, g2_ref, gpost_ref, o_ref):
    o_ref[...] = x1_ref[...] + g2_ref[...] * _rms(y_ref[...], gpost_ref[...])


def _final(x1, y, mod, per_row, rows_per_mod, g_post, tm):
    n = x1.shape[0]
    row = pl.BlockSpec((tm, D_MODEL), lambda i: (i, 0))
    return pl.pallas_call(
        _final_kernel,
        grid=(n // tm,),
        in_specs=[row, row, _mod_spec(per_row, tm, 5, rows_per_mod), _const_spec((1, D_MODEL))],
        out_specs=row,
        out_shape=jax.ShapeDtypeStruct((n, D_MODEL), F32),
        compiler_params=_cparams(("arbitrary",)),
        name="final_residual",
    )(x1, y, mod, g_post)


def _peer(h2, wts):
    ht = h2.T
    a, nn, b, r2 = _peer_route(ht, wts["w_peer_q_t"], wts["subkeys"])
    yt = _peer_dense(ht, wts["peer_u"], wts["peer_v_t"], a, nn, b, r2)
    return yt.T


def _rope_tables(pos):
    half = MLA_ROPE // 2
    inv = ROPE_THETA ** (-jnp.arange(half, dtype=F32) / half)
    ang = pos.astype(F32)[:, None] * inv[None, :]
    cos, sin = jnp.cos(ang), jnp.sin(ang)
    z = jnp.zeros((pos.shape[0], 128 - MLA_ROPE), F32)
    return jnp.concatenate([cos, cos, z], axis=1), jnp.concatenate([sin, sin, z], axis=1)


def _rot_cols(w):
    half = MLA_ROPE // 2
    return jnp.concatenate([-w[..., half:], w[..., :half]], axis=-1)


def _pad_cols(w, width):
    return jnp.pad(w, [(0, 0)] * (w.ndim - 1) + [(0, width - w.shape[-1])])


def _prep_weights(g_pre_mix, g_post_mix, g_pre_ffn, g_post_ffn, w_in, b_forget, g_cq, w_uq, g_ckv, w_uk, w_uv,
                  w_o, w_peer_q, peer_subkeys, peer_u, peer_v):
    nq = FOX_HEADS * HEAD_DIM
    nk = FOX_KV_HEADS * HEAD_DIM
    o = 0
    w_qkv = w_in[:, o:o + nq + 2 * nk]; o += nq + 2 * nk
    w_f = w_in[:, o:o + FOX_HEADS]; o += FOX_HEADS
    w_cq = w_in[:, o:o + Q_LORA]; o += Q_LORA
    w_ckv = w_in[:, o:o + KV_LORA]; o += KV_LORA
    w_kr = w_in[:, o:o + MLA_ROPE]; o += MLA_ROPE
    w_gate = w_in[:, o:]
    w_sm = jnp.concatenate([w_cq, w_ckv, _pad_cols(w_kr, 128), _pad_cols(_rot_cols(w_kr), 128), _pad_cols(w_f, 128)], axis=1)
    uq = w_uq.reshape(Q_LORA, MLA_HEADS, MLA_NOPE + MLA_ROPE)
    uq_n = uq[:, :, :MLA_NOPE].reshape(Q_LORA, MLA_HEADS * MLA_NOPE)
    uq_r = _pad_cols(uq[:, :, MLA_NOPE:], 128).reshape(Q_LORA, MLA_HEADS * 128)
    uq_rr = _pad_cols(_rot_cols(uq[:, :, MLA_NOPE:]), 128).reshape(Q_LORA, MLA_HEADS * 128)
    row = lambda g: g.reshape(1, -1)
    return dict(
        g_pre_mix=row(g_pre_mix), g_post_mix=row(g_post_mix), g_pre_ffn=row(g_pre_ffn), g_post_ffn=row(g_post_ffn),
        w_qkv=w_qkv.astype(BF16), w_gate=w_gate.astype(BF16), w_sm=w_sm.astype(BF16),
        b_forget=row(b_forget), g_cq=row(g_cq), g_ckv=row(g_ckv),
        w_uq=jnp.concatenate([uq_n, uq_r, uq_rr], axis=1).astype(BF16),
        w_uk=jnp.transpose(w_uk, (1, 2, 0)).astype(BF16),
        w_uv=jnp.transpose(w_uv, (1, 0, 2)).astype(BF16),
        w_o=w_o.astype(BF16),
        w_peer_q_t=w_peer_q.T.astype(BF16),
        subkeys=peer_subkeys.astype(BF16),
        peer_u=peer_u.astype(BF16),
        peer_v_t=peer_v.T.astype(BF16),
    )


def kernel(x_prompt, x_sample, c_prompt, c_sample, cache_fox_k, cache_fox_v, cache_fox_logf, cache_mla_ckv, cache_mla_krope, page_table, w_ada, b_ada, g_pre_mix, g_post_mix, g_pre_ffn, g_post_ffn, w_in, b_forget, g_cq, w_uq, g_ckv, w_uk, w_uv, w_o, w_peer_q, peer_subkeys, peer_u, peer_v):
    assert w_ada.shape[0] == 1, "single layer"
    batch, seq, d = x_prompt.shape
    bd, dt, _ = x_sample.shape
    n_pages = page_table.shape[1]
    n_pool = cache_fox_k.shape[1]
    wts = _prep_weights(g_pre_mix[0], g_post_mix[0], g_pre_ffn[0], g_post_ffn[0], w_in[0], b_forget[0], g_cq[0],
                        w_uq[0], g_ckv[0], w_uk[0], w_uv[0], w_o[0], w_peer_q[0], peer_subkeys[0], peer_u[0], peer_v[0])

    nc = batch + bd
    cpad = (-nc) % 8
    c_all = jnp.concatenate([c_prompt, c_sample, jnp.zeros((cpad, d), F32)], axis=0)
    mod = _adaln(c_all, w_ada[0], b_ada[0])
    mod_p = mod[:batch].reshape(batch, 1, 6 * d)
    mod_s = jnp.repeat(mod[batch:nc], dt, axis=0)

    tm = 256
    xp = x_prompt.reshape(batch * seq, d)
    cos_p, sin_p = _rope_tables(jnp.arange(seq))
    pin = _mixer_in(xp, mod_p, False, seq, cos_p, sin_p, seq // tm, wts, tm)
    fcum = jnp.cumsum(pin["logf"].reshape(batch, seq, FOX_HEADS), axis=1)
    fcol = fcum.reshape(batch * seq, FOX_KV_HEADS, FOX_GROUP).transpose(1, 0, 2)
    frow = fcum.reshape(batch, seq, FOX_KV_HEADS, FOX_GROUP).transpose(0, 2, 3, 1)
    o_fox_p = _fox_prompt(pin["qf"], pin["kb"], pin["vb"], fcol, frow, batch, seq)
    o_mla_p = _mla_prompt(pin["qcat"], pin["kcat"], wts["w_uv"], batch, seq)
    x1_p, h2_p = _mixer_out(o_fox_p, o_mla_p, pin["gate"], xp, mod_p, False, seq, wts, tm)

    ns = bd * dt
    xs = x_sample.reshape(ns, d)
    cos_s, sin_s = _rope_tables(PAST_LEN + jnp.arange(dt))
    cos_s = jnp.tile(cos_s, (ns // dt, 1))
    sin_s = jnp.tile(sin_s, (ns // dt, 1))
    tms = 256
    sin_ = _mixer_in(xs, mod_s, True, 1, cos_s, sin_s, ns // tms, wts, tms)
    pt_flat = page_table.reshape(-1)
    J, G, R = FOX_KV_HEADS, FOX_GROUP, 16
    q_s = sin_["qf"].reshape(bd, dt, J, G, HEAD_DIM).transpose(0, 2, 3, 1, 4).reshape(bd, J, G * dt, HEAD_DIM)
    q_s = jnp.pad(q_s, ((0, 0), (0, 0), (0, R - G * dt), (0, 0)))
    lfn = sin_["logf"].reshape(bd, dt, J, G).transpose(0, 2, 3, 1)
    lfn = jnp.broadcast_to(lfn[:, :, :, None, :], (bd, J, G, dt, dt)).reshape(bd, J, G * dt, dt)
    lfn = jnp.pad(lfn, ((0, 0), (0, 0), (0, R - G * dt), (0, 0))).reshape(bd, J * R, dt)
    kw = FOX_KV_HEADS * HEAD_DIM
    o_fox_s = _fox_sample(pt_flat, q_s, sin_["kf"].reshape(bd, dt, kw), sin_["vf"].reshape(bd, dt, kw), lfn,
                          cache_fox_k[0].reshape(n_pool, PAGE_SIZE, kw), cache_fox_v[0].reshape(n_pool, PAGE_SIZE, kw),
                          jnp.swapaxes(cache_fox_logf[0], 1, 2), n_pages)
    o_fox_s = o_fox_s[:, :, :G * dt].reshape(bd, J, G, dt, HEAD_DIM).transpose(0, 3, 1, 2, 4).reshape(ns, FOX_HEADS * HEAD_DIM)
    qm_s = sin_["qcat"].reshape(MLA_HEADS, bd, dt, MLA_QK).transpose(1, 0, 2, 3).reshape(bd, MLA_HEADS * dt, MLA_QK)
    kn_s = jnp.concatenate([sin_["c_kv"], sin_["k_rope"], jnp.zeros((ns, MLA_QK - KV_LORA - MLA_ROPE), F32)], axis=1)
    o_lat_s = _mla_sample(pt_flat, qm_s, kn_s.reshape(bd, dt, MLA_QK), cache_mla_ckv[0], cache_mla_krope[0], n_pages)
    o_lat_s = o_lat_s.reshape(bd, MLA_HEADS, dt, KV_LORA).transpose(1, 0, 2, 3).reshape(MLA_HEADS, ns, KV_LORA)
    o_mla_s = _uv_proj(o_lat_s, wts["w_uv"])
    x1_s, h2_s = _mixer_out(o_fox_s, o_mla_s, sin_["gate"], xs, mod_s, True, 1, wts, tms)

    y_all = _peer(jnp.concatenate([h2_p, h2_s], axis=0), wts)
    y_p = _final(x1_p, y_all[:batch * seq], mod_p, False, seq, wts["g_post_ffn"], tm)
    y_s = _final(x1_s, y_all[batch * seq:], mod_s, True, 1, wts["g_post_ffn"], tms)

    def rows(r, lead):
        return (r["kf"].reshape(1, *lead, FOX_KV_HEADS, HEAD_DIM), r["vf"].reshape(1, *lead, FOX_KV_HEADS, HEAD_DIM),
                r["logf"].reshape(1, *lead, FOX_HEADS), r["c_kv"].reshape(1, *lead, KV_LORA),
                r["k_rope"].reshape(1, *lead, MLA_ROPE))

    return (y_p.reshape(batch, seq, d), y_s.reshape(bd, dt, d)) + rows(pin, (batch, seq)) + rows(sin_, (bd, dt))
```

```python
import functools

import numpy as np
import jax
import jax.numpy as jnp
from jax import lax
from jax.experimental import pallas as pl
from jax.experimental.pallas import tpu as pltpu

D_MODEL = 2048
PAST_LEN = 8192
PAGE_SIZE = 128
FOX_HEADS = 8
FOX_KV_HEADS = 4
FOX_GROUP = FOX_HEADS // FOX_KV_HEADS
HEAD_DIM = 128
MLA_HEADS = 8
MLA_NOPE = 128
MLA_ROPE = 64
MLA_V = 128
Q_LORA = 512
KV_LORA = 512
ROPE_THETA = 10000.0
PEER_HEADS = 8
N_KEYS = 128
PEER_TOPK = 16
PEER_HALF = 128
RMS_EPS = 1e-6

FOX_SCALE = HEAD_DIM ** -0.5
MLA_SCALE = (MLA_NOPE + MLA_ROPE) ** -0.5
MLA_QK = KV_LORA + 128
NEG_BIG = -1e30

BF16 = jnp.bfloat16
F32 = jnp.float32

VMEM_LIMIT = 56 * 1024 * 1024


def _cparams(sem, vmem=VMEM_LIMIT):
    return pltpu.CompilerParams(dimension_semantics=sem, vmem_limit_bytes=vmem)


def _const_spec(shape):
    nd = len(shape)
    return pl.BlockSpec(shape, lambda *_: (0,) * nd, pipeline_mode=pl.Buffered(1))


def _rms(x, g):
    return x * lax.rsqrt(jnp.mean(x * x, axis=-1, keepdims=True) + RMS_EPS) * g


def _dot(a, b):
    return jnp.dot(a, b, preferred_element_type=F32)


def _dot_t(a, b):
    return lax.dot_general(a, b, (((1,), (1,)), ((), ())), preferred_element_type=F32)


def _ada_kernel(c_ref, w_ref, b_ref, o_ref):
    c = c_ref[...]
    a = (c * jax.nn.sigmoid(c)).astype(BF16)
    o_ref[...] = _dot(a, w_ref[...].astype(BF16)) + b_ref[...]


def _adaln(c, w_ada, b_ada, tn=1024):
    rows, d = c.shape
    n = w_ada.shape[1]
    return pl.pallas_call(
        _ada_kernel,
        grid=(n // tn,),
        in_specs=[pl.BlockSpec((rows, d), lambda j: (0, 0)),
                  pl.BlockSpec((d, tn), lambda j: (0, j)),
                  pl.BlockSpec((1, tn), lambda j: (0, j))],
        out_specs=pl.BlockSpec((rows, tn), lambda j: (0, j)),
        out_shape=jax.ShapeDtypeStruct((rows, n), F32),
        compiler_params=_cparams(("arbitrary",)),
        name="adaln",
    )(c, w_ada, b_ada.reshape(1, n))


def _in_a_kernel(x_ref, shift_ref, scale_ref, g_ref, w_ref,
                 hb_ref, qf_ref, kf_ref, vf_ref, kb_ref, vb_ref):
    x = x_ref[...]
    h = _rms(x, g_ref[...]) * (1.0 + scale_ref[...]) + shift_ref[...]
    hb = h.astype(BF16)
    hb_ref[...] = hb
    qkv = _dot(hb, w_ref[...])
    nq = qf_ref.shape[1]
    nk = kf_ref.shape[1]
    qf_ref[...] = (qkv[:, :nq] * FOX_SCALE).astype(BF16)
    kf = qkv[:, nq:nq + nk]
    vf = qkv[:, nq + nk:]
    kf_ref[...] = kf
    vf_ref[...] = vf
    kb_ref[...] = kf.astype(BF16)
    vb_ref[...] = vf.astype(BF16)


def _in_b_kernel(hb_ref, w_ref, gate_ref):
    gate_ref[...] = jax.nn.sigmoid(_dot(hb_ref[...], w_ref[...]))


def _in_c_kernel(hb_ref, w_ref, bf_ref, gcq_ref, gckv_ref, wuq_ref, wuk_ref, cos_ref, sin_ref,
                 logf_ref, ckv_ref, kr_ref, kcat_ref, qcat_ref):
    sm = _dot(hb_ref[...], w_ref[...])
    cq = sm[:, :Q_LORA]
    ckv = sm[:, Q_LORA:Q_LORA + KV_LORA]
    o = Q_LORA + KV_LORA
    kr = sm[:, o:o + 128]
    krr = sm[:, o + 128:o + 256]
    fl = sm[:, o + 256:o + 256 + FOX_HEADS]
    logf_ref[...] = jax.nn.log_sigmoid(fl + bf_ref[...])
    c_kv = _rms(ckv, gckv_ref[...])
    ckv_ref[...] = c_kv
    cos = cos_ref[...]
    sin = sin_ref[...]
    k_rope = kr * cos + krr * sin
    kr_ref[...] = k_rope[:, :MLA_ROPE]
    kcat_ref[:, :KV_LORA] = c_kv.astype(BF16)
    kcat_ref[:, KV_LORA:] = k_rope.astype(BF16)
    cqn = _rms(cq, gcq_ref[...]).astype(BF16)
    qm = _dot(cqn, wuq_ref[...])
    hn = MLA_HEADS * MLA_NOPE
    for h in range(MLA_HEADS):
        qn = qm[:, h * MLA_NOPE:(h + 1) * MLA_NOPE].astype(BF16)
        ql = _dot(qn, wuk_ref[h])
        qcat_ref[h, :, :KV_LORA] = (ql * MLA_SCALE).astype(BF16)
        qr = qm[:, hn + h * 128:hn + (h + 1) * 128]
        qrr = qm[:, 2 * hn + h * 128:2 * hn + (h + 1) * 128]
        qcat_ref[h, :, KV_LORA:] = ((qr * cos + qrr * sin) * MLA_SCALE).astype(BF16)


def _mod_spec(per_row, tm, chunk, rows_per_mod):
    if per_row:
        return pl.BlockSpec((tm, D_MODEL), lambda i: (i, chunk))
    return pl.BlockSpec((None, 1, D_MODEL), lambda i: (i * tm // rows_per_mod, 0, chunk))


def _mixer_in(x, mod, per_row, rows_per_mod, cos, sin, pos_blocks, wts, tm):
    n = x.shape[0]
    grid = (n // tm,)
    row = lambda w: pl.BlockSpec((tm, w), lambda i: (i, 0))
    sds = lambda w, dt: jax.ShapeDtypeStruct((n, w), dt)
    nqf = FOX_HEADS * HEAD_DIM
    nkf = FOX_KV_HEADS * HEAD_DIM
    hb, qf, kf, vf, kb, vb = pl.pallas_call(
        _in_a_kernel,
        grid=grid,
        in_specs=[row(D_MODEL), _mod_spec(per_row, tm, 0, rows_per_mod), _mod_spec(per_row, tm, 1, rows_per_mod),
                  _const_spec((1, D_MODEL)), _const_spec(wts["w_qkv"].shape)],
        out_specs=[row(D_MODEL), row(nqf), row(nkf), row(nkf), row(nkf), row(nkf)],
        out_shape=[sds(D_MODEL, BF16), sds(nqf, BF16), sds(nkf, F32), sds(nkf, F32), sds(nkf, BF16), sds(nkf, BF16)],
        compiler_params=_cparams(("arbitrary",)),
        name="mixer_in_a",
    )(x, mod, mod, wts["g_pre_mix"], wts["w_qkv"])
    gate = pl.pallas_call(
        _in_b_kernel,
        grid=grid,
        in_specs=[row(D_MODEL), _const_spec(wts["w_gate"].shape)],
        out_specs=row(D_MODEL),
        out_shape=sds(D_MODEL, F32),
        compiler_params=_cparams(("arbitrary",)),
        name="mixer_in_b",
    )(hb, wts["w_gate"])
    pos_spec = pl.BlockSpec((tm, 128), lambda i: (i % pos_blocks, 0))
    logf, c_kv, k_rope, kcat, qcat = pl.pallas_call(
        _in_c_kernel,
        grid=grid,
        in_specs=[row(D_MODEL), _const_spec(wts["w_sm"].shape), _const_spec((1, FOX_HEADS)),
                  _const_spec((1, Q_LORA)), _const_spec((1, KV_LORA)), _const_spec(wts["w_uq"].shape),
                  _const_spec(wts["w_uk"].shape), pos_spec, pos_spec],
        out_specs=[row(FOX_HEADS), row(KV_LORA), row(MLA_ROPE), row(MLA_QK),
                   pl.BlockSpec((MLA_HEADS, tm, MLA_QK), lambda i: (0, i, 0))],
        out_shape=[sds(FOX_HEADS, F32), sds(KV_LORA, F32), sds(MLA_ROPE, F32), sds(MLA_QK, BF16),
                   jax.ShapeDtypeStruct((MLA_HEADS, n, MLA_QK), BF16)],
        compiler_params=_cparams(("arbitrary",)),
        name="mixer_in_c",
    )(hb, wts["w_sm"], wts["b_forget"], wts["g_cq"], wts["g_ckv"], wts["w_uq"], wts["w_uk"], cos, sin)
    return dict(qf=qf, kf=kf, vf=vf, kb=kb, vb=vb, gate=gate, logf=logf, c_kv=c_kv, k_rope=k_rope,
                kcat=kcat, qcat=qcat)


def _fox_prompt_kernel(q_ref, k_ref, v_ref, fcol_ref, frow_ref, o_ref, *, tq):
    qi = pl.program_id(2)
    q = jnp.concatenate([q_ref[:, :HEAD_DIM], q_ref[:, HEAD_DIM:]], axis=0)
    fq = [fcol_ref[:, g:g + 1] for g in range(FOX_GROUP)]

    def step(kb, carry, masked):
        start = pl.multiple_of(kb * tq, tq)
        k = k_ref[pl.ds(start, tq), :]
        v = v_ref[pl.ds(start, tq), :]
        s = _dot_t(q, k)
        ps, new = [], []
        for g in range(FOX_GROUP):
            m, l = carry[2 * g], carry[2 * g + 1]
            sg = s[g * tq:(g + 1) * tq] + fq[g] - frow_ref[g:g + 1, pl.ds(start, tq)]
            if masked:
                r = lax.broadcasted_iota(jnp.int32, (tq, tq), 0)
                c = lax.broadcasted_iota(jnp.int32, (tq, tq), 1)
                sg = jnp.where(c <= r, sg, NEG_BIG)
            m_new = jnp.maximum(m, jnp.max(sg, axis=1, keepdims=True))
            corr = jnp.exp(m - m_new)
            p = jnp.exp(sg - m_new)
            new += [m_new, l * corr + jnp.sum(p, axis=1, keepdims=True), corr]
            ps.append(p.astype(BF16))
        pv = _dot(jnp.concatenate(ps, axis=0), v)
        acc = carry[4]
        corr = jnp.concatenate([jnp.broadcast_to(new[2], (tq, HEAD_DIM)), jnp.broadcast_to(new[5], (tq, HEAD_DIM))], axis=0)
        return (new[0], new[1], new[3], new[4], acc * corr + pv)

    init = (jnp.full((tq, 1), NEG_BIG, F32), jnp.zeros((tq, 1), F32),
            jnp.full((tq, 1), NEG_BIG, F32), jnp.zeros((tq, 1), F32),
            jnp.zeros((FOX_GROUP * tq, HEAD_DIM), F32))
    carry = lax.fori_loop(0, qi, lambda kb, c: step(kb, c, False), init)
    m0, l0, m1, l1, acc = step(qi, carry, True)
    o_ref[:, :HEAD_DIM] = acc[:tq] / l0
    o_ref[:, HEAD_DIM:] = acc[tq:] / l1


def _fox_prompt(qf, kb, vb, fcol, frow, batch, seq, tq=256):
    n = qf.shape[0]
    nq = seq // tq
    gw = FOX_GROUP * HEAD_DIM
    return pl.pallas_call(
        functools.partial(_fox_prompt_kernel, tq=tq),
        grid=(batch, FOX_KV_HEADS, nq),
        in_specs=[pl.BlockSpec((tq, gw), lambda b, j, i: (b * nq + i, j)),
                  pl.BlockSpec((seq, HEAD_DIM), lambda b, j, i: (b, j)),
                  pl.BlockSpec((seq, HEAD_DIM), lambda b, j, i: (b, j)),
                  pl.BlockSpec((None, tq, FOX_GROUP), lambda b, j, i: (j, b * nq + i, 0)),
                  pl.BlockSpec((None, None, FOX_GROUP, seq), lambda b, j, i: (b, j, 0, 0))],
        out_specs=pl.BlockSpec((tq, gw), lambda b, j, i: (b * nq + i, j)),
        out_shape=jax.ShapeDtypeStruct((n, FOX_HEADS * HEAD_DIM), F32),
        compiler_params=_cparams(("arbitrary", "arbitrary", "arbitrary")),
        name="fox_prompt",
    )(qf, kb, vb, fcol, frow)


def _mla_prompt_kernel(q_ref, k_ref, wuv_ref, o_ref, m_ref, l_ref, acc_ref, *, tq, tk):
    qi = pl.program_id(1)
    rows = MLA_HEADS * tq
    q = q_ref[...].reshape(rows, MLA_QK)
    m_ref[...] = jnp.full(m_ref.shape, NEG_BIG, F32)
    l_ref[...] = jnp.zeros(l_ref.shape, F32)
    acc_ref[...] = jnp.zeros(acc_ref.shape, F32)

    def step(kb, masked):
        start = pl.multiple_of(kb * tk, tk)
        k = k_ref[pl.ds(start, tk), :]
        s = _dot_t(q, k)
        if masked:
            r = lax.broadcasted_iota(jnp.int32, (rows, tk), 0)
            c = lax.broadcasted_iota(jnp.int32, (rows, tk), 1)
            qpos = qi * tq + (r & (tq - 1))
            s = jnp.where(start + c <= qpos, s, NEG_BIG)
        m = m_ref[...]
        m_new = jnp.maximum(m, jnp.max(s, axis=1, keepdims=True))
        corr = jnp.exp(m - m_new)
        p = jnp.exp(s - m_new)
        l_ref[...] = l_ref[...] * corr + jnp.sum(p, axis=1, keepdims=True)
        m_ref[...] = m_new
        acc_ref[...] = acc_ref[...] * corr + _dot(p.astype(BF16), k[:, :KV_LORA])

    last = (qi * tq) // tk

    def body(kb, _):
        step(kb, False)
        return 0

    lax.fori_loop(0, last, body, 0)
    step(last, True)
    o = acc_ref[...] / l_ref[...]
    for h in range(MLA_HEADS):
        oh = o[h * tq:(h + 1) * tq].astype(BF16)
        o_ref[:, h * MLA_V:(h + 1) * MLA_V] = _dot(oh, wuv_ref[h])


def _mla_prompt(qcat, kcat, wuv, batch, seq, tq=128, tk=256):
    n = kcat.shape[0]
    nq = seq // tq
    rows = MLA_HEADS * tq
    return pl.pallas_call(
        functools.partial(_mla_prompt_kernel, tq=tq, tk=tk),
        grid=(batch, nq),
        in_specs=[pl.BlockSpec((MLA_HEADS, tq, MLA_QK), lambda b, i: (0, b * nq + i, 0)),
                  pl.BlockSpec((seq, MLA_QK), lambda b, i: (b, 0)),
                  _const_spec(wuv.shape)],
        out_specs=pl.BlockSpec((tq, MLA_HEADS * MLA_V), lambda b, i: (b * nq + i, 0)),
        out_shape=jax.ShapeDtypeStruct((n, MLA_HEADS * MLA_V), F32),
        scratch_shapes=[pltpu.VMEM((rows, 1), F32), pltpu.VMEM((rows, 1), F32), pltpu.VMEM((rows, KV_LORA), F32)],
        compiler_params=_cparams(("arbitrary", "arbitrary")),
        name="mla_prompt",
    )(qcat, kcat, wuv)


def _fox_sample_kernel(pt_ref, q_ref, kn_ref, vn_ref, lfn_ref, *refs, pages):
    k_refs = refs[:pages]
    v_refs = refs[pages:2 * pages]
    lf_refs = refs[2 * pages:3 * pages]
    o_ref = refs[3 * pages]
    m_ref, l_ref, acc_ref, suf_ref = refs[3 * pages + 1:]
    g = pl.program_id(1)
    J, R = FOX_KV_HEADS, 16
    T = kn_ref.shape[0]

    lfn = lfn_ref[...]
    gc = [lfn[:, 0:1]]
    for s in range(1, T):
        gc.append(gc[-1] + lfn[:, s:s + 1])
    trow = lax.broadcasted_iota(jnp.int32, (J * R, 1), 0) & (T - 1)
    grow = gc[T - 1]
    for t in range(T - 2, -1, -1):
        grow = jnp.where(trow == t, gc[t], grow)

    @pl.when(g == 0)
    def _():
        suf_ref[...] = jnp.zeros(suf_ref.shape, F32)
        for j in range(J):
            qj = q_ref[j].astype(F32)
            gr = grow[j * R:(j + 1) * R]
            ss = []
            for s in range(T):
                ks = kn_ref[s:s + 1, j * HEAD_DIM:(j + 1) * HEAD_DIM]
                sc = jnp.sum(qj * ks, axis=1, keepdims=True) + gr - gc[s][j * R:(j + 1) * R]
                ss.append(jnp.where(trow[j * R:(j + 1) * R] >= s, sc, NEG_BIG))
            m = ss[0]
            for s in range(1, T):
                m = jnp.maximum(m, ss[s])
            l = jnp.zeros((R, 1), F32)
            acc = jnp.zeros((R, HEAD_DIM), F32)
            for s in range(T):
                p = jnp.exp(ss[s] - m)
                l = l + p
                acc = acc + p * vn_ref[s:s + 1, j * HEAD_DIM:(j + 1) * HEAD_DIM]
            m_ref[j] = m
            l_ref[j] = l
            acc_ref[j] = acc

    suffix = suf_ref[...]
    lane = lax.broadcasted_iota(jnp.int32, (FOX_HEADS, PAGE_SIZE), 1)
    rsub = lax.broadcasted_iota(jnp.int32, (R, PAGE_SIZE), 0)
    biases = []
    for r in range(pages):
        lft = lf_refs[r][...]
        pre = lft
        k = 1
        while k < PAGE_SIZE:
            pre = pre + jnp.where(lane >= k, pltpu.roll(pre, k, axis=1), 0.0)
            k *= 2
        total = pre[:, PAGE_SIZE - 1:PAGE_SIZE]
        rin = (total - pre) + suffix
        suffix = suffix + total
        per_head = []
        for j in range(J):
            b0 = jnp.broadcast_to(rin[FOX_GROUP * j:FOX_GROUP * j + 1, :], (R, PAGE_SIZE))
            b1 = jnp.broadcast_to(rin[FOX_GROUP * j + 1:FOX_GROUP * j + 2, :], (R, PAGE_SIZE))
            per_head.append(jnp.where(rsub < T, b0, jnp.where(rsub < FOX_GROUP * T, b1, 0.0)) + grow[j * R:(j + 1) * R])
        biases.append(per_head)
    suf_ref[...] = suffix

    def head_rows(ref, j):
        return ref[pl.ds(j, PAGE_SIZE, stride=J), :].astype(BF16)

    for j in range(J):
        q = q_ref[j]
        ss = [_dot_t(q, head_rows(k_refs[r], j)) + biases[r][j] for r in range(pages)]
        mx = ss[0]
        for r in range(1, pages):
            mx = jnp.maximum(mx, ss[r])
        m = m_ref[j]
        m_new = jnp.maximum(m, jnp.max(mx, axis=1, keepdims=True))
        corr = jnp.exp(m - m_new)
        psum, pv = None, None
        for r in range(pages):
            p = jnp.exp(ss[r] - m_new)
            d = _dot(p.astype(BF16), head_rows(v_refs[r], j))
            psum = p if psum is None else psum + p
            pv = d if pv is None else pv + d
        l_ref[j] = l_ref[j] * corr + jnp.sum(psum, axis=1, keepdims=True)
        m_ref[j] = m_new
        acc_ref[j] = acc_ref[j] * corr + pv

    @pl.when(g == pl.num_programs(1) - 1)
    def _():
        for j in range(J):
            o_ref[j] = acc_ref[j] / l_ref[j]


def _fox_sample(pt_flat, q, kn, vn, lfn, cache_k, cache_v, cache_lf, layer, n_pages, pages=16):
    bd = q.shape[0]
    J, R = FOX_KV_HEADS, 16
    T = kn.shape[1]
    ng = n_pages // pages
    assert FOX_GROUP == 2 and FOX_GROUP * T <= R
    nl, n_pool = cache_k.shape[:2]
    cache_k = cache_k.reshape(nl, n_pool, PAGE_SIZE * J, HEAD_DIM)
    cache_v = cache_v.reshape(nl, n_pool, PAGE_SIZE * J, HEAD_DIM)
    cache_lf = jnp.swapaxes(cache_lf, 2, 3)

    def page_map(r):
        return lambda b, g, pt: (layer, pt[b * n_pages + (n_pages - 1 - (g * pages + r))], 0, 0)

    kw = FOX_KV_HEADS * HEAD_DIM
    in_specs = [pl.BlockSpec((None, J, R, HEAD_DIM), lambda b, g, pt: (b, 0, 0, 0)),
                pl.BlockSpec((None, T, kw), lambda b, g, pt: (b, 0, 0)),
                pl.BlockSpec((None, T, kw), lambda b, g, pt: (b, 0, 0)),
                pl.BlockSpec((None, J * R, T), lambda b, g, pt: (b, 0, 0))]
    in_specs += [pl.BlockSpec((None, None, PAGE_SIZE * J, HEAD_DIM), page_map(r)) for r in range(pages)]
    in_specs += [pl.BlockSpec((None, None, PAGE_SIZE * J, HEAD_DIM), page_map(r)) for r in range(pages)]
    in_specs += [pl.BlockSpec((None, None, FOX_HEADS, PAGE_SIZE), page_map(r)) for r in range(pages)]
    grid_spec = pltpu.PrefetchScalarGridSpec(
        num_scalar_prefetch=1,
        grid=(bd, ng),
        in_specs=in_specs,
        out_specs=pl.BlockSpec((None, J, R, HEAD_DIM), lambda b, g, pt: (b, 0, 0, 0)),
        scratch_shapes=[pltpu.VMEM((J, R, 1), F32), pltpu.VMEM((J, R, 1), F32), pltpu.VMEM((J, R, HEAD_DIM), F32),
                        pltpu.VMEM((FOX_HEADS, 1), F32)],
    )
    return pl.pallas_call(
        functools.partial(_fox_sample_kernel, pages=pages),
        grid_spec=grid_spec,
        out_shape=jax.ShapeDtypeStruct((bd, J, R, HEAD_DIM), F32),
        compiler_params=_cparams(("arbitrary", "arbitrary")),
        name="fox_sample",
    )(pt_flat, q, kn, vn, lfn, *([cache_k] * pages), *([cache_v] * pages), *([cache_lf] * pages))


def _mla_sample_kernel(pt_ref, q_ref, kn_ref, *refs, pages):
    c_refs = refs[:pages]
    r_refs = refs[pages:2 * pages]
    o_ref = refs[2 * pages]
    m_ref, l_ref, acc_ref = refs[2 * pages + 1:]
    g = pl.program_id(1)
    T = kn_ref.shape[0]
    rows = q_ref.shape[0]
    q = q_ref[...]

    @pl.when(g == 0)
    def _():
        qf = q.astype(F32)
        trow = lax.broadcasted_iota(jnp.int32, (rows, 1), 0) & (T - 1)
        ss = []
        for s in range(T):
            sc = jnp.sum(qf * kn_ref[s:s + 1, :], axis=1, keepdims=True)
            ss.append(jnp.where(trow >= s, sc, NEG_BIG))
        m = ss[0]
        for s in range(1, T):
            m = jnp.maximum(m, ss[s])
        l = jnp.zeros((rows, 1), F32)
        acc = jnp.zeros((rows, KV_LORA), F32)
        for s in range(T):
            p = jnp.exp(ss[s] - m)
            l = l + p
            acc = acc + p * kn_ref[s:s + 1, :KV_LORA]
        m_ref[...] = m
        l_ref[...] = l
        acc_ref[...] = acc

    ql = q[:, :KV_LORA]
    qr = q[:, KV_LORA:KV_LORA + MLA_ROPE]
    ckvs = [c_refs[r][...].astype(BF16) for r in range(pages)]
    ss = [_dot_t(ql, ckvs[r]) + _dot(qr, r_refs[r][...].astype(BF16)) for r in range(pages)]
    mx = ss[0]
    for r in range(1, pages):
        mx = jnp.maximum(mx, ss[r])
    m = m_ref[...]
    m_new = jnp.maximum(m, jnp.max(mx, axis=1, keepdims=True))
    corr = jnp.exp(m - m_new)
    psum, pv = None, None
    for r in range(pages):
        p = jnp.exp(ss[r] - m_new)
        d = _dot(p.astype(BF16), ckvs[r])
        psum = p if psum is None else psum + p
        pv = d if pv is None else pv + d
    l_ref[...] = l_ref[...] * corr + jnp.sum(psum, axis=1, keepdims=True)
    m_ref[...] = m_new
    acc_ref[...] = acc_ref[...] * corr + pv

    @pl.when(g == pl.num_programs(1) - 1)
    def _():
        o_ref[...] = acc_ref[...] / l_ref[...]


def _mla_sample(pt_flat, q, kn, cache_ckv, cache_kr, layer, n_pages, pages=16):
    bd, rows, _ = q.shape
    T = kn.shape[1]
    ng = n_pages // pages
    cache_kr = jnp.swapaxes(cache_kr, 2, 3)

    def page_map(r):
        return lambda b, g, pt: (layer, pt[b * n_pages + g * pages + r], 0, 0)

    in_specs = [pl.BlockSpec((None, rows, MLA_QK), lambda b, g, pt: (b, 0, 0)),
                pl.BlockSpec((None, T, MLA_QK), lambda b, g, pt: (b, 0, 0))]
    in_specs += [pl.BlockSpec((None, None, PAGE_SIZE, KV_LORA), page_map(r)) for r in range(pages)]
    in_specs += [pl.BlockSpec((None, None, MLA_ROPE, PAGE_SIZE), page_map(r)) for r in range(pages)]
    grid_spec = pltpu.PrefetchScalarGridSpec(
        num_scalar_prefetch=1,
        grid=(bd, ng),
        in_specs=in_specs,
        out_specs=pl.BlockSpec((None, rows, KV_LORA), lambda b, g, pt: (b, 0, 0)),
        scratch_shapes=[pltpu.VMEM((rows, 1), F32), pltpu.VMEM((rows, 1), F32), pltpu.VMEM((rows, KV_LORA), F32)],
    )
    return pl.pallas_call(
        functools.partial(_mla_sample_kernel, pages=pages),
        grid_spec=grid_spec,
        out_shape=jax.ShapeDtypeStruct((bd, rows, KV_LORA), F32),
        compiler_params=_cparams(("arbitrary", "arbitrary")),
        name="mla_sample",
    )(pt_flat, q, kn, *([cache_ckv] * pages), *([cache_kr] * pages))


def _fox_sample_attn(qf, kf, vf, logf, cache_k, cache_v, cache_lf, page_table, layer, bd, dt, pages=16):
    J, G, R = FOX_KV_HEADS, FOX_GROUP, 16
    n_pages = page_table.shape[1]
    q = qf.reshape(bd, dt, J, G, HEAD_DIM).transpose(0, 2, 3, 1, 4).reshape(bd, J, G * dt, HEAD_DIM)
    q = jnp.pad(q, ((0, 0), (0, 0), (0, R - G * dt), (0, 0)))
    lfn = logf.reshape(bd, dt, J, G).transpose(0, 2, 3, 1)
    lfn = jnp.broadcast_to(lfn[:, :, :, None, :], (bd, J, G, dt, dt)).reshape(bd, J, G * dt, dt)
    lfn = jnp.pad(lfn, ((0, 0), (0, 0), (0, R - G * dt), (0, 0))).reshape(bd, J * R, dt)
    kw = J * HEAD_DIM
    o = _fox_sample(page_table.reshape(-1), q, kf.reshape(bd, dt, kw), vf.reshape(bd, dt, kw), lfn,
                    cache_k, cache_v, cache_lf, layer, n_pages, pages)
    return o[:, :, :G * dt].reshape(bd, J, G, dt, HEAD_DIM).transpose(0, 3, 1, 2, 4).reshape(bd * dt, FOX_HEADS * HEAD_DIM)


def _mla_sample_attn(qcat, c_kv, k_rope, cache_ckv, cache_kr, page_table, layer, bd, dt, pages=16):
    ns = bd * dt
    n_pages = page_table.shape[1]
    q = qcat.reshape(MLA_HEADS, bd, dt, MLA_QK).transpose(1, 0, 2, 3).reshape(bd, MLA_HEADS * dt, MLA_QK)
    kn = jnp.concatenate([c_kv, k_rope, jnp.zeros((ns, MLA_QK - KV_LORA - MLA_ROPE), F32)], axis=1)
    o = _mla_sample(page_table.reshape(-1), q, kn.reshape(bd, dt, MLA_QK), cache_ckv, cache_kr, layer, n_pages, pages)
    return o.reshape(bd, MLA_HEADS, dt, KV_LORA).transpose(1, 0, 2, 3).reshape(MLA_HEADS, ns, KV_LORA)


def _uv_kernel(o_ref, w_ref, out_ref):
    for h in range(MLA_HEADS):
        out_ref[:, h * MLA_V:(h + 1) * MLA_V] = _dot(o_ref[h].astype(BF16), w_ref[h])


def _uv_proj(o_lat, wuv):
    _, n, c = o_lat.shape
    return pl.pallas_call(
        _uv_kernel,
        grid=(1,),
        in_specs=[pl.BlockSpec(o_lat.shape, lambda i: (0, 0, 0)), pl.BlockSpec(wuv.shape, lambda i: (0, 0, 0))],
        out_specs=pl.BlockSpec((n, MLA_HEADS * MLA_V), lambda i: (0, 0)),
        out_shape=jax.ShapeDtypeStruct((n, MLA_HEADS * MLA_V), F32),
        compiler_params=_cparams(("arbitrary",)),
        name="mla_uv",
    )(o_lat, wuv)


def _mixer_out_kernel(of_ref, om_ref, gate_ref, x_ref, g1_ref, sc2_ref, sh2_ref, gpost_ref, gpre_ref, wo_ref,
                      x1_ref, h2_ref):
    nf = of_ref.shape[1]
    mf = (of_ref[...] * gate_ref[:, :nf]).astype(BF16)
    mm = (om_ref[...] * gate_ref[:, nf:]).astype(BF16)
    y = _dot(mf, wo_ref[:nf, :]) + _dot(mm, wo_ref[nf:, :])
    x1 = x_ref[...] + g1_ref[...] * _rms(y, gpost_ref[...])
    x1_ref[...] = x1
    h2 = _rms(x1, gpre_ref[...]) * (1.0 + sc2_ref[...]) + sh2_ref[...]
    h2_ref[...] = h2.astype(BF16)


def _mixer_out(o_fox, o_mla, gate, x, mod, per_row, rows_per_mod, wts, tm):
    n = x.shape[0]
    row = lambda w: pl.BlockSpec((tm, w), lambda i: (i, 0))
    ms = lambda c: _mod_spec(per_row, tm, c, rows_per_mod)
    return pl.pallas_call(
        _mixer_out_kernel,
        grid=(n // tm,),
        in_specs=[row(o_fox.shape[1]), row(o_mla.shape[1]), row(D_MODEL), row(D_MODEL), ms(2), ms(4), ms(3),
                  _const_spec((1, D_MODEL)), _const_spec((1, D_MODEL)), _const_spec(wts["w_o"].shape)],
        out_specs=[row(D_MODEL), row(D_MODEL)],
        out_shape=[jax.ShapeDtypeStruct((n, D_MODEL), F32), jax.ShapeDtypeStruct((n, D_MODEL), BF16)],
        compiler_params=_cparams(("arbitrary",)),
        name="mixer_out",
    )(o_fox, o_mla, gate, x, mod, mod, mod, wts["g_post_mix"], wts["g_pre_ffn"], wts["w_o"])


_CAND_LIMITS = [PEER_TOPK // (a + 1) for a in range(PEER_TOPK)]


def _topk_rows(s, k):
    n = s.shape[0]
    row = lax.broadcasted_iota(jnp.int32, s.shape, 0).astype(F32)
    rank = jnp.full(s.shape, float(k), F32)
    work = s
    vals = []
    for i in range(k):
        m = jnp.max(work, axis=0, keepdims=True)
        idx = jnp.min(jnp.where(work == m, row, float(n)), axis=0, keepdims=True)
        hit = row == idx
        rank = jnp.where(hit, float(i), rank)
        work = jnp.where(hit, -jnp.inf, work)
        vals.append(m)
    return vals, rank


def _peer_route_kernel(ht_ref, wq_ref, sk_ref, a_ref, n_ref, b_ref, r2_ref):
    qt = _dot(wq_ref[...], ht_ref[...]).astype(BF16)
    t = qt.shape[1]
    for h in range(PEER_HEADS):
        s1 = _dot(sk_ref[h, 0], qt[(2 * h) * PEER_HALF:(2 * h + 1) * PEER_HALF])
        s2 = _dot(sk_ref[h, 1], qt[(2 * h + 1) * PEER_HALF:(2 * h + 2) * PEER_HALF])
        v1, r1 = _topk_rows(s1, PEER_TOPK)
        v2, r2 = _topk_rows(s2, PEER_TOPK)
        v2a = jnp.concatenate(v2[:8], axis=0)
        v2b = jnp.concatenate(v2[8:], axis=0)
        brow = lax.broadcasted_iota(jnp.int32, (8, t), 0)
        groups = [v1[0] + v2a, v1[0] + v2b]
        for a in range(1, PEER_TOPK):
            groups.append(jnp.where(brow < _CAND_LIMITS[a], v1[a] + v2a, -jnp.inf))
        cand = jnp.concatenate(groups, axis=0)
        cmax = groups[0][0:1]
        _, rc = _topk_rows(cand, PEER_TOPK)
        chosen = rc < float(PEER_TOPK)
        z = jnp.sum(jnp.where(chosen, jnp.exp(cand - cmax), 0.0), axis=0, keepdims=True)
        cnt = jnp.where(chosen, 1.0, 0.0)
        nkey = jnp.zeros(s1.shape, F32)
        for a in range(PEER_TOPK):
            lo = 0 if a == 0 else 8 * (a + 1)
            hi = 16 if a == 0 else lo + 8
            na = jnp.sum(cnt[lo:hi], axis=0, keepdims=True)
            nkey = nkey + jnp.where(r1 == float(a), na, 0.0)
        a_ref[h] = jnp.where(r1 < float(PEER_TOPK), jnp.exp(s1 - v1[0]), 0.0)
        n_ref[h] = nkey
        b_ref[h] = (jnp.where(r2 < float(PEER_TOPK), jnp.exp(s2 - v2[0]), 0.0) / z).astype(BF16)
        r2_ref[h] = r2.astype(BF16)


def _peer_route(ht, wq_t, subkeys, tt=256):
    d, n = ht.shape
    shp = jax.ShapeDtypeStruct((PEER_HEADS, N_KEYS, n), F32)
    shp_b = jax.ShapeDtypeStruct((PEER_HEADS, N_KEYS, n), BF16)
    spec = pl.BlockSpec((PEER_HEADS, N_KEYS, tt), lambda i: (0, 0, i))
    return pl.pallas_call(
        _peer_route_kernel,
        grid=(n // tt,),
        in_specs=[pl.BlockSpec((d, tt), lambda i: (0, i)), _const_spec(wq_t.shape), _const_spec(subkeys.shape)],
        out_specs=[spec, spec, spec, spec],
        out_shape=[shp, shp, shp_b, shp_b],
        compiler_params=_cparams(("arbitrary",)),
        name="peer_route",
    )(ht, wq_t, subkeys)


def _peer_dense_kernel(ht_ref, u_ref, vt_ref, a_ref, n_ref, b_ref, r2_ref, y_ref, xu0_ref, xu1_ref, w0_ref, w1_ref,
                       *, ne):
    e = pl.program_id(1)
    tt = ht_ref.shape[1]
    xu_refs = (xu0_ref, xu1_ref)
    w_refs = (w0_ref, w1_ref)
    ni = u_ref.shape[0] // N_KEYS
    sqrt_half = float(np.sqrt(0.5))

    def rows(ref, h, ii):
        row = jnp.broadcast_to(ref[h, ii:ii + 1, :], (16, tt)).astype(BF16)
        return jnp.concatenate([row] * (N_KEYS // 16), axis=0)

    def step(do1, do2, do3, par):
        if do1:
            xu_refs[par][...] = _dot(u_ref[...], ht_ref[...])
        if do2:
            xu_ref, w_ref = xu_refs[1 - par], w_refs[1 - par]
            zero = jnp.zeros((N_KEYS, tt), BF16)
            for ii in range(ni):
                c = zero
                for h in range(PEER_HEADS):
                    c = c + jnp.where(r2_ref[h] < rows(n_ref, h, ii), b_ref[h], zero) * rows(a_ref, h, ii)
                xu = xu_ref[ii * N_KEYS:(ii + 1) * N_KEYS, :]
                act = 0.5 * xu * (1.0 + lax.erf(xu * sqrt_half))
                w_ref[ii * N_KEYS:(ii + 1) * N_KEYS, :] = act.astype(BF16) * c
        if do3:
            y_ref[...] += _dot(vt_ref[...], w_refs[par][...])

    @pl.when(e == 0)
    def _():
        y_ref[...] = jnp.zeros(y_ref.shape, F32)
        step(True, False, False, 0)

    @pl.when(e == 1)
    def _():
        step(True, True, False, 1)

    for par in (0, 1):
        @pl.when((e >= 2) & (e < ne) & ((e & 1) == par))
        def _():
            step(True, True, True, par)

    @pl.when(e == ne)
    def _():
        step(False, True, True, ne & 1)

    @pl.when(e == ne + 1)
    def _():
        step(False, False, True, (ne + 1) & 1)


def _peer_dense(ht, u, vt, a, cnt, b, r2, tt=512, te=1024):
    d, n = ht.shape
    ne = u.shape[0] // te
    ni = te // N_KEYS
    clip = lambda x: jnp.minimum(jnp.maximum(x, 0), ne - 1)
    return pl.pallas_call(
        functools.partial(_peer_dense_kernel, ne=ne),
        grid=(n // tt, ne + 2),
        in_specs=[pl.BlockSpec((d, tt), lambda t, e: (0, t)),
                  pl.BlockSpec((te, d), lambda t, e: (clip(e), 0)),
                  pl.BlockSpec((d, te), lambda t, e: (0, clip(e - 2))),
                  pl.BlockSpec((PEER_HEADS, ni, tt), lambda t, e: (0, clip(e - 1), t)),
                  pl.BlockSpec((PEER_HEADS, ni, tt), lambda t, e: (0, clip(e - 1), t)),
                  pl.BlockSpec((PEER_HEADS, N_KEYS, tt), lambda t, e: (0, 0, t)),
                  pl.BlockSpec((PEER_HEADS, N_KEYS, tt), lambda t, e: (0, 0, t))],
        out_specs=pl.BlockSpec((d, tt), lambda t, e: (0, t)),
        out_shape=jax.ShapeDtypeStruct((d, n), F32),
        scratch_shapes=[pltpu.VMEM((te, tt), F32), pltpu.VMEM((te, tt), F32),
                        pltpu.VMEM((te, tt), BF16), pltpu.VMEM((te, tt), BF16)],
        compiler_params=_cparams(("arbitrary", "arbitrary")),
        name="peer_dense",
    )(ht, u, vt, a, cnt, b, r2)


def _final_kernel(x1_ref, y_ref, g2_ref, gpost_ref, o_ref):
    o_ref[...] = x1_ref[...] + g2_ref[...] * _rms(y_ref[...], gpost_ref[...])


def _final(x1, y, mod, per_row, rows_per_mod, g_post, tm):
    n = x1.shape[0]
    row = pl.BlockSpec((tm, D_MODEL), lambda i: (i, 0))
    return pl.pallas_call(
        _final_kernel,
        grid=(n // tm,),
        in_specs=[row, row, _mod_spec(per_row, tm, 5, rows_per_mod), _const_spec((1, D_MODEL))],
        out_specs=row,
        out_shape=jax.ShapeDtypeStruct((n, D_MODEL), F32),
        compiler_params=_cparams(("arbitrary",)),
        name="final_residual",
    )(x1, y, mod, g_post)


def _peer(h2, wts):
    ht = h2.T
    a, nn, b, r2 = _peer_route(ht, wts["w_peer_q_t"], wts["subkeys"])
    yt = _peer_dense(ht, wts["peer_u"], wts["peer_v_t"], a, nn, b, r2)
    return yt.T


def _rope_tables(pos):
    half = MLA_ROPE // 2
    inv = ROPE_THETA ** (-jnp.arange(half, dtype=F32) / half)
    ang = pos.astype(F32)[:, None] * inv[None, :]
    cos, sin = jnp.cos(ang), jnp.sin(ang)
    z = jnp.zeros((pos.shape[0], 128 - MLA_ROPE), F32)
    return jnp.concatenate([cos, cos, z], axis=1), jnp.concatenate([sin, sin, z], axis=1)


def _rot_cols(w):
    half = MLA_ROPE // 2
    return jnp.concatenate([-w[..., half:], w[..., :half]], axis=-1)


def _pad_cols(w, width):
    return jnp.pad(w, [(0, 0)] * (w.ndim - 1) + [(0, width - w.shape[-1])])


def _prep_weights(g_pre_mix, g_post_mix, g_pre_ffn, g_post_ffn, w_in, b_forget, g_cq, w_uq, g_ckv, w_uk, w_uv,
                  w_o, w_peer_q, peer_subkeys, peer_u, peer_v):
    nq = FOX_HEADS * HEAD_DIM
    nk = FOX_KV_HEADS * HEAD_DIM
    o = 0
    w_qkv = w_in[:, o:o + nq + 2 * nk]; o += nq + 2 * nk
    w_f = w_in[:, o:o + FOX_HEADS]; o += FOX_HEADS
    w_cq = w_in[:, o:o + Q_LORA]; o += Q_LORA
    w_ckv = w_in[:, o:o + KV_LORA]; o += KV_LORA
    w_kr = w_in[:, o:o + MLA_ROPE]; o += MLA_ROPE
    w_gate = w_in[:, o:]
    w_sm = jnp.concatenate([w_cq, w_ckv, _pad_cols(w_kr, 128), _pad_cols(_rot_cols(w_kr), 128), _pad_cols(w_f, 128)], axis=1)
    uq = w_uq.reshape(Q_LORA, MLA_HEADS, MLA_NOPE + MLA_ROPE)
    uq_n = uq[:, :, :MLA_NOPE].reshape(Q_LORA, MLA_HEADS * MLA_NOPE)
    uq_r = _pad_cols(uq[:, :, MLA_NOPE:], 128).reshape(Q_LORA, MLA_HEADS * 128)
    uq_rr = _pad_cols(_rot_cols(uq[:, :, MLA_NOPE:]), 128).reshape(Q_LORA, MLA_HEADS * 128)
    row = lambda g: g.reshape(1, -1)
    return dict(
        g_pre_mix=row(g_pre_mix), g_post_mix=row(g_post_mix), g_pre_ffn=row(g_pre_ffn), g_post_ffn=row(g_post_ffn),
        w_qkv=w_qkv.astype(BF16), w_gate=w_gate.astype(BF16), w_sm=w_sm.astype(BF16),
        b_forget=row(b_forget), g_cq=row(g_cq), g_ckv=row(g_ckv),
        w_uq=jnp.concatenate([uq_n, uq_r, uq_rr], axis=1).astype(BF16),
        w_uk=jnp.transpose(w_uk, (1, 2, 0)).astype(BF16),
        w_uv=jnp.transpose(w_uv, (1, 0, 2)).astype(BF16),
        w_o=w_o.astype(BF16),
        w_peer_q_t=w_peer_q.T.astype(BF16),
        subkeys=peer_subkeys.astype(BF16),
        peer_u=peer_u.astype(BF16),
        peer_v_t=peer_v.T.astype(BF16),
    )


def kernel(x_prompt, x_sample, c_prompt, c_sample, cache_fox_k, cache_fox_v, cache_fox_logf, cache_mla_ckv, cache_mla_krope, page_table, w_ada, b_ada, g_pre_mix, g_post_mix, g_pre_ffn, g_post_ffn, w_in, b_forget, g_cq, w_uq, g_ckv, w_uk, w_uv, w_o, w_peer_q, peer_subkeys, peer_u, peer_v):
    assert w_ada.shape[0] == 1, "single layer"
    batch, seq, d = x_prompt.shape
    bd, dt, _ = x_sample.shape
    wts = _prep_weights(g_pre_mix[0], g_post_mix[0], g_pre_ffn[0], g_post_ffn[0], w_in[0], b_forget[0], g_cq[0],
                        w_uq[0], g_ckv[0], w_uk[0], w_uv[0], w_o[0], w_peer_q[0], peer_subkeys[0], peer_u[0], peer_v[0])

    nc = batch + bd
    cpad = (-nc) % 8
    c_all = jnp.concatenate([c_prompt, c_sample, jnp.zeros((cpad, d), F32)], axis=0)
    mod = _adaln(c_all, w_ada[0], b_ada[0])
    mod_p = mod[:batch].reshape(batch, 1, 6 * d)
    mod_s = jnp.repeat(mod[batch:nc], dt, axis=0)

    tm = 256
    xp = x_prompt.reshape(batch * seq, d)
    cos_p, sin_p = _rope_tables(jnp.arange(seq))
    pin = _mixer_in(xp, mod_p, False, seq, cos_p, sin_p, seq // tm, wts, tm)
    fcum = jnp.cumsum(pin["logf"].reshape(batch, seq, FOX_HEADS), axis=1)
    fcol = fcum.reshape(batch * seq, FOX_KV_HEADS, FOX_GROUP).transpose(1, 0, 2)
    frow = fcum.reshape(batch, seq, FOX_KV_HEADS, FOX_GROUP).transpose(0, 2, 3, 1)
    o_fox_p = _fox_prompt(pin["qf"], pin["kb"], pin["vb"], fcol, frow, batch, seq)
    o_mla_p = _mla_prompt(pin["qcat"], pin["kcat"], wts["w_uv"], batch, seq)
    x1_p, h2_p = _mixer_out(o_fox_p, o_mla_p, pin["gate"], xp, mod_p, False, seq, wts, tm)

    ns = bd * dt
    xs = x_sample.reshape(ns, d)
    cos_s, sin_s = _rope_tables(PAST_LEN + jnp.arange(dt))
    cos_s = jnp.tile(cos_s, (ns // dt, 1))
    sin_s = jnp.tile(sin_s, (ns // dt, 1))
    tms = 256
    sin_ = _mixer_in(xs, mod_s, True, 1, cos_s, sin_s, ns // tms, wts, tms)
    o_fox_s = _fox_sample_attn(sin_["qf"], sin_["kf"], sin_["vf"], sin_["logf"], cache_fox_k, cache_fox_v,
                               cache_fox_logf, page_table, 0, bd, dt)
    o_lat_s = _mla_sample_attn(sin_["qcat"], sin_["c_kv"], sin_["k_rope"], cache_mla_ckv, cache_mla_krope,
                               page_table, 0, bd, dt)
    o_mla_s = _uv_proj(o_lat_s, wts["w_uv"])
    x1_s, h2_s = _mixer_out(o_fox_s, o_mla_s, sin_["gate"], xs, mod_s, True, 1, wts, tms)

    y_all = _peer(jnp.concatenate([h2_p, h2_s], axis=0), wts)
    y_p = _final(x1_p, y_all[:batch * seq], mod_p, False, seq, wts["g_post_ffn"], tm)
    y_s = _final(x1_s, y_all[batch * seq:], mod_s, True, 1, wts["g_post_ffn"], tms)

    def rows(r, lead):
        return (r["kf"].reshape(1, *lead, FOX_KV_HEADS, HEAD_DIM), r["vf"].reshape(1, *lead, FOX_KV_HEADS, HEAD_DIM),
                r["logf"].reshape(1, *lead, FOX_HEADS), r["c_kv"].reshape(1, *lead, KV_LORA),
                r["k_rope"].reshape(1, *lead, MLA_ROPE))

    return (y_p.reshape(batch, seq, d), y_s.reshape(bd, dt, d)) + rows(pin, (batch, seq)) + rows(sin_, (bd, dt))
```
